```python
import jax, jax.numpy as jnp
from jax import lax
import numpy as np

D_MODEL = 1024
BATCH = 16
SEQ = 2048
DEPTH = 4

N_BRANCH = 4
MIX_W = D_MODEL // N_BRANCH
HEAD_DIM = 64
CONV_A_K = 31
NA_HEADS = MIX_W // HEAD_DIM
NA_ROWS = 8
NA_COLS = 16
GRID_W = 64
DIL_HEADS = MIX_W // HEAD_DIM
DIL_PAIRS = ((128, 1), (512, 4), (2048, 16))
SWA_Q_HEADS = MIX_W // HEAD_DIM
SWA_KV_HEADS = SWA_Q_HEADS // 2
SWA_WINDOW = 128
Q_BLOCK = 128
ROPE_THETA = 500000.0
ROPE_DIMS = HEAD_DIM // 4
D_FF = 2816
FFN_CONV_K = 3
EPS = 1e-6
NEG_INF = -1e30

IN_SIZES = (
    N_BRANCH * D_MODEL,
    2 * MIX_W,
    3 * NA_HEADS * HEAD_DIM,
    3 * DIL_HEADS * HEAD_DIM,
    SWA_Q_HEADS * HEAD_DIM,
    2 * SWA_KV_HEADS * HEAD_DIM,
)
N_IN = sum(IN_SIZES)
IN_SPLITS = [int(s) for s in np.cumsum(IN_SIZES)[:-1]]

kernel_name = 'hybrid_gated_parallel_encoder'


def _rmsnorm(x, g):
    xf = x.astype(jnp.float32)
    y = xf * lax.rsqrt(jnp.mean(jnp.square(xf), axis=-1, keepdims=True) + EPS)
    return (y * g.astype(jnp.float32)).astype(x.dtype)


def _layernorm(x, g, b):
    xf = x.astype(jnp.float32)
    mu = jnp.mean(xf, axis=-1, keepdims=True)
    var = jnp.mean(jnp.square(xf - mu), axis=-1, keepdims=True)
    y = (xf - mu) * lax.rsqrt(var + EPS)
    return (y * g.astype(jnp.float32) + b.astype(jnp.float32)).astype(x.dtype)


def _rope_tables(seq):
    pos = jnp.arange(seq, dtype=jnp.float32)
    inv = ROPE_THETA ** (-jnp.arange(0, ROPE_DIMS, 2, dtype=jnp.float32) / ROPE_DIMS)
    ang = pos[:, None] * inv[None, :]
    return jnp.cos(ang), jnp.sin(ang)


def _partial_rope(x, cos, sin):
    half = ROPE_DIMS // 2
    xf = x.astype(jnp.float32)
    x1, x2, rest = xf[..., :half], xf[..., half:ROPE_DIMS], xf[..., ROPE_DIMS:]
    c, s = cos[None, :, None, :], sin[None, :, None, :]
    out = jnp.concatenate([x1 * c - x2 * s, x2 * c + x1 * s, rest], axis=-1)
    return out.astype(x.dtype)


def _depthwise_conv(u, w, b):
    k, ch = w.shape
    p = (k - 1) // 2
    y = lax.conv_general_dilated(u, w[:, None, :].astype(u.dtype), (1,), [(p, p)],
                                 dimension_numbers=('NWC', 'WIO', 'NWC'),
                                 feature_group_count=ch)
    return y + b.astype(u.dtype)


def _conformer_conv(a_in, conv_w, conv_b, ln_g, ln_b):
    a, g = jnp.split(a_in, 2, axis=-1)
    u = a * jax.nn.sigmoid(g)
    u = _depthwise_conv(u, conv_w, conv_b)
    u = _layernorm(u, ln_g, ln_b)
    return jax.nn.silu(u)


def _neighbourhood_attn(q, k, v, rpb):
    bsz, seq, heads, dh = q.shape
    rows = seq // GRID_W
    kr = min(NA_ROWS, rows)
    kc = min(NA_COLS, GRID_W)
    qg = q.reshape(bsz, rows, GRID_W, heads, dh)
    kg = k.reshape(bsz, rows, GRID_W, heads, dh)
    vg = v.reshape(bsz, rows, GRID_W, heads, dh)
    col = np.arange(GRID_W)
    col_start = np.clip(col - kc // 2, 0, GRID_W - kc)
    col_idx = col_start[:, None] + np.arange(kc)[None, :]
    dc = col_idx - col[:, None] + (NA_COLS - 1)
    bias_c = rpb[:, :, dc].astype(jnp.float32)
    scale = dh ** -0.5

    def row_block(r):
        row_start = jnp.clip(r - kr // 2, 0, rows - kr)
        row_idx = row_start + jnp.arange(kr)
        dr = row_idx - r + (NA_ROWS - 1)
        q_r = lax.dynamic_index_in_dim(qg, r, axis=1, keepdims=False)
        k_r = jnp.take(kg, row_idx, axis=1)[:, :, col_idx]
        v_r = jnp.take(vg, row_idx, axis=1)[:, :, col_idx]
        s = jnp.einsum('bwhd,brwkhd->bhwrk', q_r, k_r).astype(jnp.float32) * scale
        bias = jnp.transpose(jnp.take(bias_c, dr, axis=1), (0, 2, 1, 3))
        s = s + bias[None]
        p = jax.nn.softmax(s.reshape(bsz, heads, GRID_W, kr * kc), axis=-1)
        p = p.reshape(bsz, heads, GRID_W, kr, kc).astype(v.dtype)
        return jnp.einsum('bhwrk,brwkhd->bwhd', p, v_r)

    out = lax.map(row_block, jnp.arange(rows))
    return jnp.transpose(out, (1, 0, 2, 3, 4)).reshape(bsz, seq, heads * dh)


def _dilated_group(q, k, v, dil, n_side):
    bsz, seq, heads, dh = q.shape
    offs = dil * np.arange(-n_side, n_side + 1)
    scale = dh ** -0.5

    def block(i):
        start = i * Q_BLOCK
        kpos = start + jnp.arange(Q_BLOCK)[:, None] + offs[None, :]
        valid = (kpos >= 0) & (kpos < seq)
        kidx = jnp.clip(kpos, 0, seq - 1)
        q_b = lax.dynamic_slice_in_dim(q, start, Q_BLOCK, axis=1)
        k_b = jnp.take(k, kidx, axis=1)
        v_b = jnp.take(v, kidx, axis=1)
        s = jnp.einsum('bqhd,bqjhd->bhqj', q_b, k_b).astype(jnp.float32) * scale
        s = jnp.where(valid[None, None], s, NEG_INF)
        m = jnp.max(s, axis=-1, keepdims=True)
        p = jnp.exp(s - m)
        l = jnp.sum(p, axis=-1)
        o = jnp.einsum('bhqj,bqjhd->bqhd', p, v_b.astype(jnp.float32))
        o = o / jnp.transpose(l, (0, 2, 1))[..., None]
        lse = jnp.transpose(m[..., 0] + jnp.log(l), (0, 2, 1))
        return o, lse

    o, lse = lax.map(block, jnp.arange(seq // Q_BLOCK))
    o = jnp.transpose(o, (1, 0, 2, 3, 4)).reshape(bsz, seq, heads, dh)
    lse = jnp.transpose(lse, (1, 0, 2, 3)).reshape(bsz, seq, heads)
    return o, lse


def _dilated_attn(q, k, v):
    bsz, seq, heads, dh = q.shape
    outs, lses = [], []
    for window, dil in DIL_PAIRS:
        o, lse = _dilated_group(q, k, v, dil, window // (2 * dil))
        outs.append(o)
        lses.append(lse)
    wts = jax.nn.softmax(jnp.stack(lses, axis=0), axis=0)
    out = jnp.sum(wts[..., None] * jnp.stack(outs, axis=0), axis=0)
    return out.reshape(bsz, seq, heads * dh).astype(q.dtype)


def _window_gqa_sink(q, k, v, sink):
    bsz, seq, hq, dh = q.shape
    hkv = k.shape[2]
    grp = hq // hkv
    pad = SWA_WINDOW
    klen = Q_BLOCK + 2 * pad
    kp = jnp.pad(k, ((0, 0), (pad, pad), (0, 0), (0, 0)))
    vp = jnp.pad(v, ((0, 0), (pad, pad), (0, 0), (0, 0)))
    qg = q.reshape(bsz, seq, hkv, grp, dh)
    sink_g = sink.astype(jnp.float32).reshape(hkv, grp)[None, :, :, None]
    scale = dh ** -0.5

    def block(i):
        start = i * Q_BLOCK
        q_b = lax.dynamic_slice_in_dim(qg, start, Q_BLOCK, axis=1)
        k_b = lax.dynamic_slice_in_dim(kp, start, klen, axis=1)
        v_b = lax.dynamic_slice_in_dim(vp, start, klen, axis=1)
        qpos = start + jnp.arange(Q_BLOCK)
        kpos = start - pad + jnp.arange(klen)
        valid = (jnp.abs(qpos[:, None] - kpos[None, :]) <= SWA_WINDOW) & (kpos >= 0)[None] & (kpos < seq)[None]
        s = jnp.einsum('bqkgd,bskd->bkgqs', q_b, k_b).astype(jnp.float32) * scale
        s = jnp.where(valid[None, None, None], s, NEG_INF)
        m = jnp.maximum(jnp.max(s, axis=-1), sink_g)
        p = jnp.exp(s - m[..., None])
        denom = jnp.sum(p, axis=-1) + jnp.exp(sink_g - m)
        o = jnp.einsum('bkgqs,bskd->bqkgd', p, v_b.astype(jnp.float32))
        o = o / jnp.transpose(denom, (0, 3, 1, 2))[..., None]
        return o.reshape(bsz, Q_BLOCK, hq * dh).astype(q.dtype)

    out = lax.map(block, jnp.arange(seq // Q_BLOCK))
    return jnp.transpose(out, (1, 0, 2, 3)).reshape(bsz, seq, hq * dh)


def _conv_ffn(h, w_up, conv_w, conv_b, w_down):
    u = _depthwise_conv(h @ w_up, conv_w, conv_b)
    gate, up = jnp.split(u, 2, axis=-1)
    return (jax.nn.silu(gate) * up) @ w_down


def setup_inputs(seed: int = 0) -> dict:
    key = jax.random.key(seed)
    ks = jax.random.split(key, 24)
    f32 = jnp.float32

    def nrm(k, shape, scale):
        return jax.random.normal(k, shape, f32) * scale

    def gain(k, shape):
        return 1.0 + 0.02 * jax.random.normal(k, shape, f32)

    return {
        'x': nrm(ks[0], (BATCH, SEQ, D_MODEL), 1.0),
        'g_mix': gain(ks[1], (DEPTH, D_MODEL)),
        'w_in': nrm(ks[2], (DEPTH, D_MODEL, N_IN), D_MODEL ** -0.5),
        'gate_b': nrm(ks[3], (DEPTH, N_BRANCH, D_MODEL), 0.01),
        'a_conv_w': nrm(ks[4], (DEPTH, CONV_A_K, MIX_W), CONV_A_K ** -0.5),
        'a_conv_b': nrm(ks[5], (DEPTH, MIX_W), 0.01),
        'a_ln_g': gain(ks[6], (DEPTH, MIX_W)),
        'a_ln_b': nrm(ks[7], (DEPTH, MIX_W), 0.01),
        'na_qn': gain(ks[8], (DEPTH, HEAD_DIM)),
        'na_kn': gain(ks[9], (DEPTH, HEAD_DIM)),
        'na_rpb': nrm(ks[10], (DEPTH, NA_HEADS, 2 * NA_ROWS - 1, 2 * NA_COLS - 1), 0.1),
        'dil_qn': gain(ks[11], (DEPTH, HEAD_DIM)),
        'dil_kn': gain(ks[12], (DEPTH, HEAD_DIM)),
        'swa_qn': gain(ks[13], (DEPTH, HEAD_DIM)),
        'swa_kn': gain(ks[14], (DEPTH, HEAD_DIM)),
        'swa_sink': nrm(ks[15], (DEPTH, SWA_Q_HEADS), 1.0),
        'w_branch': nrm(ks[16], (DEPTH, N_BRANCH, MIX_W, D_MODEL), MIX_W ** -0.5),
        'w_out': nrm(ks[17], (DEPTH, D_MODEL, D_MODEL), D_MODEL ** -0.5),
        'g_ffn': gain(ks[18], (DEPTH, D_MODEL)),
        'w_up': nrm(ks[19], (DEPTH, D_MODEL, 2 * D_FF), D_MODEL ** -0.5),
        'ffn_conv_w': nrm(ks[20], (DEPTH, FFN_CONV_K, 2 * D_FF), FFN_CONV_K ** -0.5),
        'ffn_conv_b': nrm(ks[21], (DEPTH, 2 * D_FF), 0.01),
        'w_down': nrm(ks[22], (DEPTH, D_FF, D_MODEL), D_FF ** -0.5),
    }


def reference(x, g_mix, w_in, gate_b, a_conv_w, a_conv_b, a_ln_g, a_ln_b,
              na_qn, na_kn, na_rpb, dil_qn, dil_kn, swa_qn, swa_kn, swa_sink,
              w_branch, w_out, g_ffn, w_up, ffn_conv_w, ffn_conv_b, w_down):
    bsz, seq, _ = x.shape
    cos, sin = _rope_tables(seq)
    for l in range(DEPTH):
        h = _rmsnorm(x, g_mix[l])
        proj = h @ w_in[l]
        gate_pre, a_in, na_qkv, dil_qkv, swa_q, swa_kv = jnp.split(proj, IN_SPLITS, axis=-1)

        y_a = _conformer_conv(a_in, a_conv_w[l], a_conv_b[l], a_ln_g[l], a_ln_b[l])

        bq, bk, bv = [t.reshape(bsz, seq, NA_HEADS, HEAD_DIM) for t in jnp.split(na_qkv, 3, axis=-1)]
        y_b = _neighbourhood_attn(_rmsnorm(bq, na_qn[l]), _rmsnorm(bk, na_kn[l]), bv, na_rpb[l])

        cq, ck, cv = [t.reshape(bsz, seq, DIL_HEADS, HEAD_DIM) for t in jnp.split(dil_qkv, 3, axis=-1)]
        cq = _partial_rope(_rmsnorm(cq, dil_qn[l]), cos, sin)
        ck = _partial_rope(_rmsnorm(ck, dil_kn[l]), cos, sin)
        y_c = _dilated_attn(cq, ck, cv)

        dq = swa_q.reshape(bsz, seq, SWA_Q_HEADS, HEAD_DIM)
        dk, dv = [t.reshape(bsz, seq, SWA_KV_HEADS, HEAD_DIM) for t in jnp.split(swa_kv, 2, axis=-1)]
        dq = _partial_rope(_rmsnorm(dq, swa_qn[l]), cos, sin)
        dk = _partial_rope(_rmsnorm(dk, swa_kn[l]), cos, sin)
        y_d = _window_gqa_sink(dq, dk, dv, swa_sink[l])

        ys = jnp.stack([y_a, y_b, y_c, y_d], axis=2)
        branch = jnp.einsum('bsnc,ncd->bsnd', ys, w_branch[l])
        gates = jax.nn.sigmoid(gate_pre.reshape(bsz, seq, N_BRANCH, D_MODEL) + gate_b[l])
        mixed = jnp.sum(gates * branch, axis=2)
        x = x + mixed @ w_out[l]

        x = x + _conv_ffn(_rmsnorm(x, g_ffn[l]), w_up[l], ffn_conv_w[l], ffn_conv_b[l], w_down[l])
    return x
```

```python
import functools

import numpy as np
import jax
import jax.numpy as jnp
from jax import lax
from jax.experimental import pallas as pl
from jax.experimental.pallas import tpu as pltpu

F32 = jnp.float32
BF16 = jnp.bfloat16

N_BRANCH = 4
HEAD_DIM = 64
CONV_A_K = 31
NA_ROWS = 8
NA_COLS = 16
GRID_W = 64
DIL_PAIRS = ((128, 1), (512, 4), (2048, 16))
SWA_WINDOW = 128
Q_BLOCK = 128
ROPE_THETA = 500000.0
ROPE_DIMS = HEAD_DIM // 4
FFN_CONV_K = 3
EPS = 1e-6
NEG_INF = -1e30

LANES = 128
SUBLANES = 8
VMEM_LIMIT = 56 * 1024 * 1024

HEADS_PER_SLAB = LANES // HEAD_DIM


def _cparams(*sem):
    return pltpu.CompilerParams(dimension_semantics=sem, vmem_limit_bytes=VMEM_LIMIT)


def _low_head_mask():
    return lax.broadcasted_iota(jnp.int32, (1, LANES), 1) < HEAD_DIM


def _rmsnorm_rows(x, g):
    return x * lax.rsqrt(jnp.mean(x * x, axis=-1, keepdims=True) + EPS) * g


def _head_rmsnorm(slab, gain, blockdiag):
    sq = slab * slab
    hi = sq.astype(BF16)
    lo = (sq - hi.astype(F32)).astype(BF16)
    ss = (jnp.dot(hi, blockdiag, preferred_element_type=F32)
          + jnp.dot(lo, blockdiag, preferred_element_type=F32))
    return slab * lax.rsqrt(ss * (1.0 / HEAD_DIM) + EPS) * gain


def _rope(y, rope_ref):
    half = ROPE_DIMS // 2
    return (y * rope_ref[0]
            + pltpu.roll(y, LANES - half, 1) * rope_ref[1]
            + pltpu.roll(y, half, 1) * rope_ref[2])


def _inproj_kernel(x_ref, g_ref, w_ref, gains_ref, rope_ref,
                   a_ref, naq_ref, nak_ref, nav_ref, dil_ref, swq_ref, swk_ref, swv_ref, *, mix_w):
    hb = _rmsnorm_rows(x_ref[...], g_ref[...]).astype(BF16)
    r = lax.broadcasted_iota(jnp.int32, (LANES, LANES), 0) // HEAD_DIM
    c = lax.broadcasted_iota(jnp.int32, (LANES, LANES), 1) // HEAD_DIM
    blockdiag = (r == c).astype(BF16)

    def proj(seg):
        return jnp.dot(hb, w_ref[:, seg * mix_w:(seg + 1) * mix_w], preferred_element_type=F32)

    def slabs(v):
        return [v[:, s * LANES:(s + 1) * LANES] for s in range(mix_w // LANES)]

    def gain(row, s):
        return gains_ref[row:row + 1, s * LANES:(s + 1) * LANES]

    a_ref[...] = proj(0) * jax.nn.sigmoid(proj(1))
    for s, v in enumerate(slabs(proj(2))):
        naq_ref[:, s * LANES:(s + 1) * LANES] = _head_rmsnorm(v, gain(0, s), blockdiag).astype(BF16)
    for s, v in enumerate(slabs(proj(3))):
        nak_ref[:, s * LANES:(s + 1) * LANES] = _head_rmsnorm(v, gain(1, s), blockdiag).astype(BF16)
    nav_ref[...] = proj(4).astype(BF16)
    n_slab = mix_w // LANES
    for s, v in enumerate(slabs(proj(5))):
        dil_ref[s] = _rope(_head_rmsnorm(v, gain(2, s), blockdiag), rope_ref)
    for s, v in enumerate(slabs(proj(6))):
        dil_ref[n_slab + s] = _rope(_head_rmsnorm(v, gain(3, s), blockdiag), rope_ref)
    for s, v in enumerate(slabs(proj(7))):
        dil_ref[2 * n_slab + s] = v
    for s, v in enumerate(slabs(proj(8))):
        swq_ref[:, s * LANES:(s + 1) * LANES] = _rope(_head_rmsnorm(v, gain(4, s), blockdiag), rope_ref).astype(BF16)
    for s, v in enumerate(slabs(proj(9))):
        swk_ref[:, s * LANES:(s + 1) * LANES] = _rope(_head_rmsnorm(v, gain(5, s), blockdiag), rope_ref).astype(BF16)
    swv_ref[...] = proj(10).astype(BF16)


def _inproj(x2, g, w_br, gains, rope_tab, *, seq, tm):
    tokens, d_model = x2.shape
    n_br = w_br.shape[1]
    mix_w = gains.shape[1]
    n_slab = mix_w // LANES
    seq_tiles = seq // tm
    tok_spec = lambda w: pl.BlockSpec((tm, w), lambda i: (i, 0))
    full = lambda shape: pl.BlockSpec(shape, lambda i: (0,) * len(shape))
    out_shape = (
        jax.ShapeDtypeStruct((tokens, mix_w), F32),
        jax.ShapeDtypeStruct((tokens, mix_w), BF16),
        jax.ShapeDtypeStruct((tokens, mix_w), BF16),
        jax.ShapeDtypeStruct((tokens, mix_w), BF16),
        jax.ShapeDtypeStruct((3 * n_slab, tokens, LANES), F32),
        jax.ShapeDtypeStruct((tokens, mix_w), BF16),
        jax.ShapeDtypeStruct((tokens, mix_w), BF16),
        jax.ShapeDtypeStruct((tokens, mix_w), BF16),
    )
    out_specs = (
        tok_spec(mix_w), tok_spec(mix_w), tok_spec(mix_w), tok_spec(mix_w),
        pl.BlockSpec((3 * n_slab, tm, LANES), lambda i: (0, i, 0)),
        tok_spec(mix_w), tok_spec(mix_w), tok_spec(mix_w),
    )
    return pl.pallas_call(
        functools.partial(_inproj_kernel, mix_w=mix_w),
        grid=(tokens // tm,),
        in_specs=[
            tok_spec(d_model),
            full((1, d_model)),
            full((d_model, n_br)),
            full(gains.shape),
            pl.BlockSpec((3, tm, LANES), lambda i: (0, i % seq_tiles, 0)),
        ],
        out_specs=out_specs,
        out_shape=out_shape,
        compiler_params=_cparams("parallel"),
        name="inproj",
    )(x2, g, w_br, gains, rope_tab)


CONV_CHUNK = 64
CONV_PAD = 16


def _confconv_kernel(u_ref, w_ref, b_ref, g_ref, beta_ref, o_ref, pad_ref):
    seq, ch = u_ref.shape
    half = (CONV_A_K - 1) // 2
    zeros = jnp.zeros((CONV_PAD, ch), F32)
    pad_ref[0:CONV_PAD, :] = zeros
    pad_ref[CONV_PAD + seq:2 * CONV_PAD + seq, :] = zeros
    pad_ref[CONV_PAD:CONV_PAD + seq, :] = u_ref[...]

    lead = CONV_PAD - half
    n_shift = (lead + CONV_A_K + SUBLANES - 1) // SUBLANES

    def chunk(ci, carry):
        base = pl.multiple_of(ci * CONV_CHUNK, CONV_CHUNK)
        win = pad_ref[pl.ds(base, CONV_CHUNK + n_shift * SUBLANES), :]
        acc = jnp.zeros((CONV_CHUNK, ch), F32)
        for b in range(SUBLANES):
            z = None
            for a in range(n_shift):
                k = a * SUBLANES + b - lead
                if 0 <= k < CONV_A_K:
                    term = win[a * SUBLANES:(a + 1) * SUBLANES + CONV_CHUNK] * w_ref[k:k + 1, :]
                    z = term if z is None else z + term
            acc = acc + z[b:b + CONV_CHUNK]
        acc = acc + b_ref[...]
        mu = jnp.mean(acc, axis=-1, keepdims=True)
        d = acc - mu
        var = jnp.mean(d * d, axis=-1, keepdims=True)
        y = d * lax.rsqrt(var + EPS) * g_ref[...] + beta_ref[...]
        o_ref[pl.ds(base, CONV_CHUNK), :] = (y * jax.nn.sigmoid(y)).astype(o_ref.dtype)
        return carry

    lax.fori_loop(0, seq // CONV_CHUNK, chunk, 0)


def _confconv(u, w, b, g, beta, *, seq):
    tokens, ch = u.shape
    full = lambda shape: pl.BlockSpec(shape, lambda i: (0,) * len(shape))
    return pl.pallas_call(
        _confconv_kernel,
        grid=(tokens // seq,),
        in_specs=[pl.BlockSpec((seq, ch), lambda i: (i, 0)),
                  full(w.shape), full(b.shape), full(g.shape), full(beta.shape)],
        out_specs=pl.BlockSpec((seq, ch), lambda i: (i, 0)),
        out_shape=jax.ShapeDtypeStruct((tokens, ch), BF16),
        scratch_shapes=[pltpu.VMEM((seq + 2 * CONV_PAD, ch), F32)],
        compiler_params=_cparams("parallel"),
        name="confconv",
    )(u, w, b, g, beta)


def _scores(q_slab, k_slab, low, first):
    zero = jnp.zeros_like(q_slab)
    qm = jnp.where(low, q_slab, zero) if first else jnp.where(low, zero, q_slab)
    return lax.dot_general(qm, k_slab, (((1,), (1,)), ((), ())), preferred_element_type=F32)


def _na_kernel(q_ref, k_ref, v_ref, bias_ref, o_ref, *, rows, n_slab):
    low = _low_head_mask()
    kr = NA_ROWS
    n_edge_lo = kr // 2 + 1
    mid_hi = rows - (kr - kr // 2)

    def row(r, carry):
        row_start = jnp.clip(r - kr // 2, 0, rows - kr)
        cls = jnp.where(r < n_edge_lo, r, jnp.where(r <= mid_hi, n_edge_lo, r - mid_hi + n_edge_lo))
        q0 = pl.multiple_of(r * GRID_W, GRID_W)
        k0 = pl.multiple_of(row_start * GRID_W, GRID_W)
        for s in range(n_slab):
            lanes = slice(s * LANES, (s + 1) * LANES)
            qs = q_ref[pl.ds(q0, GRID_W), lanes]
            ks = k_ref[pl.ds(k0, kr * GRID_W), lanes]
            vs = v_ref[pl.ds(k0, kr * GRID_W), lanes]
            outs = []
            for hh in range(HEADS_PER_SLAB):
                sc = _scores(qs, ks, low, hh == 0) + bias_ref[cls, s * HEADS_PER_SLAB + hh]
                m = jnp.max(sc, axis=-1, keepdims=True)
                p = jnp.exp(sc - m)
                l = jnp.sum(p, axis=-1, keepdims=True)
                pv = jnp.dot(p.astype(BF16), vs, preferred_element_type=F32)
                outs.append(pv / l)
            o_ref[pl.ds(q0, GRID_W), lanes] = jnp.where(low, outs[0], outs[1]).astype(o_ref.dtype)
        return carry

    lax.fori_loop(0, rows, row, 0)


def _na_bias_table(rpb, rows):
    kr, kc = NA_ROWS, NA_COLS
    n_edge_lo = kr // 2 + 1
    mid_hi = rows - (kr - kr // 2)
    rep_rows = list(range(n_edge_lo)) + [n_edge_lo] + list(range(mid_hi + 1, rows))
    rep = np.asarray(rep_rows)
    row_start = np.clip(rep - kr // 2, 0, rows - kr)
    dr = row_start[:, None] + np.arange(kr)[None, :] - rep[:, None] + (NA_ROWS - 1)
    col = np.arange(GRID_W)
    col_start = np.clip(col - kc // 2, 0, GRID_W - kc)
    key_col = np.arange(GRID_W)
    valid = (key_col[None, :] >= col_start[:, None]) & (key_col[None, :] < col_start[:, None] + kc)
    dc = np.clip(key_col[None, :] - col[:, None] + (NA_COLS - 1), 0, 2 * NA_COLS - 2)
    tab = rpb.astype(F32)[:, dr[:, :, None, None], dc[None, None, :, :]]
    tab = jnp.where(valid[None, None, None], tab, NEG_INF)
    tab = jnp.transpose(tab, (1, 0, 3, 2, 4))
    return tab.reshape(len(rep_rows), rpb.shape[0], GRID_W, kr * GRID_W)


def _na_attn(q, k, v, bias, *, seq):
    tokens, mix_w = q.shape
    seq_spec = pl.BlockSpec((seq, mix_w), lambda i: (i, 0))
    return pl.pallas_call(
        functools.partial(_na_kernel, rows=seq // GRID_W, n_slab=mix_w // LANES),
        grid=(tokens // seq,),
        in_specs=[seq_spec, seq_spec, seq_spec,
                  pl.BlockSpec(bias.shape, lambda i: (0, 0, 0, 0))],
        out_specs=seq_spec,
        out_shape=jax.ShapeDtypeStruct((tokens, mix_w), BF16),
        compiler_params=_cparams("parallel"),
        name="na_attn",
    )(q, k, v, bias)


def _band_mask(q0, k0, nq, nk, side):
    qi = q0 + lax.broadcasted_iota(jnp.int32, (nq, nk), 0)
    ki = k0 + lax.broadcasted_iota(jnp.int32, (nq, nk), 1)
    return jnp.abs(qi - ki) <= side


def _dil_group_block(q, k, v, valid, low):
    accs, ms, ls = [], [], []
    qb, kb, vb = q.astype(BF16), k.astype(BF16), v.astype(BF16)
    for hh in range(HEADS_PER_SLAB):
        sc = jnp.where(valid, _scores(qb, kb, low, hh == 0), NEG_INF)
        m = jnp.max(sc, axis=-1, keepdims=True)
        p = jnp.exp(sc - m)
        ls.append(jnp.sum(p, axis=-1, keepdims=True))
        ms.append(m)
        accs.append(jnp.dot(p.astype(BF16), vb, preferred_element_type=F32))
    return (jnp.where(low, accs[0], accs[1]), jnp.where(low, ms[0], ms[1]), jnp.where(low, ls[0], ls[1]))


def _dilated_kernel(qkv_ref, o_ref, acc_ref, m_ref, l_ref, *, n_slab):
    seq = qkv_ref.shape[1]
    low = _low_head_mask()
    (_, dil1), (win2, dil2), (win3, dil3) = DIL_PAIRS
    side = DIL_PAIRS[0][0] // (2 * dil1)
    qb = Q_BLOCK

    for s in range(n_slab):
        q_s, k_s, v_s = s, n_slab + s, 2 * n_slab + s

        len3 = seq // dil3

        def g3(b, carry):
            rows = pl.ds(b, len3, stride=dil3)
            valid = _band_mask(0, 0, len3, len3, side)
            acc, m, l = _dil_group_block(qkv_ref[q_s, rows, :], qkv_ref[k_s, rows, :],
                                         qkv_ref[v_s, rows, :], valid, low)
            acc_ref[1, rows, :] = acc
            m_ref[1, rows, :] = m
            l_ref[1, rows, :] = l
            return carry

        lax.fori_loop(0, dil3, g3, 0)

        len2 = seq // dil2
        nblk2 = len2 // qb

        def g2(t, carry):
            b = t // nblk2
            j = t % nblk2
            a0 = j * qb
            k0 = jnp.clip(a0 - side, 0, len2 - 2 * qb)
            qrows = pl.ds(b + dil2 * a0, qb, stride=dil2)
            krows = pl.ds(b + dil2 * k0, 2 * qb, stride=dil2)
            valid = _band_mask(a0, k0, qb, 2 * qb, side)
            acc, m, l = _dil_group_block(qkv_ref[q_s, qrows, :], qkv_ref[k_s, krows, :],
                                         qkv_ref[v_s, krows, :], valid, low)
            acc_ref[0, qrows, :] = acc
            m_ref[0, qrows, :] = m
            l_ref[0, qrows, :] = l
            return carry

        lax.fori_loop(0, dil2 * nblk2, g2, 0)

        def g1(j, carry):
            a0 = pl.multiple_of(j * qb, qb)
            k0 = pl.multiple_of(jnp.clip(a0 - side, 0, seq - 2 * qb), side)
            qrows = pl.ds(a0, qb)
            krows = pl.ds(k0, 2 * qb)
            valid = _band_mask(a0, k0, qb, 2 * qb, side)
            acc1, m1, l1 = _dil_group_block(qkv_ref[q_s, qrows, :], qkv_ref[k_s, krows, :],
                                            qkv_ref[v_s, krows, :], valid, low)
            m2, m3 = m_ref[0, qrows, :], m_ref[1, qrows, :]
            m = jnp.maximum(jnp.maximum(m1, m2), m3)
            w1, w2, w3 = jnp.exp(m1 - m), jnp.exp(m2 - m), jnp.exp(m3 - m)
            num = w1 * acc1 + w2 * acc_ref[0, qrows, :] + w3 * acc_ref[1, qrows, :]
            den = w1 * l1 + w2 * l_ref[0, qrows, :] + w3 * l_ref[1, qrows, :]
            o_ref[qrows, s * LANES:(s + 1) * LANES] = (num / den).astype(o_ref.dtype)
            return carry

        lax.fori_loop(0, seq // qb, g1, 0)


def _dilated_attn(qkv, *, seq, mix_w):
    n3, tokens, _ = qkv.shape
    n_slab = n3 // 3
    return pl.pallas_call(
        functools.partial(_dilated_kernel, n_slab=n_slab),
        grid=(tokens // seq,),
        in_specs=[pl.BlockSpec((n3, seq, LANES), lambda i: (0, i, 0))],
        out_specs=pl.BlockSpec((seq, mix_w), lambda i: (i, 0)),
        out_shape=jax.ShapeDtypeStruct((tokens, mix_w), BF16),
        scratch_shapes=[pltpu.VMEM((2, seq, LANES), F32)] * 3,
        compiler_params=_cparams("parallel"),
        name="dilated_attn",
    )(qkv)


def _swa_kernel(sink_ref, q_ref, k_ref, v_ref, o_ref, *, n_slab):
    seq = q_ref.shape[0]
    low = _low_head_mask()
    qb = Q_BLOCK
    nk = qb + 2 * SWA_WINDOW

    def block(j, carry):
        a0 = pl.multiple_of(j * qb, qb)
        k0 = pl.multiple_of(jnp.clip(a0 - SWA_WINDOW, 0, seq - nk), qb)
        valid = _band_mask(a0, k0, qb, nk, SWA_WINDOW)
        for s in range(n_slab):
            lanes = slice(s * LANES, (s + 1) * LANES)
            qs = q_ref[pl.ds(a0, qb), lanes]
            ks = k_ref[pl.ds(k0, nk), lanes]
            vs = v_ref[pl.ds(k0, nk), lanes]
            outs = []
            for hh in range(HEADS_PER_SLAB):
                sink = sink_ref[s * HEADS_PER_SLAB + hh]
                sc = jnp.where(valid, _scores(qs, ks, low, hh == 0), NEG_INF)
                m = jnp.maximum(jnp.max(sc, axis=-1, keepdims=True), sink)
                p = jnp.exp(sc - m)
                denom = jnp.sum(p, axis=-1, keepdims=True) + jnp.exp(sink - m)
                outs.append(jnp.dot(p.astype(BF16), vs, preferred_element_type=F32) / denom)
            o_ref[pl.ds(a0, qb), lanes] = jnp.where(low, outs[0], outs[1]).astype(o_ref.dtype)
        return carry

    lax.fori_loop(0, seq // qb, block, 0)


def _swa_attn(sink, q, k, v, *, seq):
    tokens, mix_w = q.shape
    seq_spec = pl.BlockSpec((seq, mix_w), lambda i: (i, 0))
    return pl.pallas_call(
        functools.partial(_swa_kernel, n_slab=mix_w // LANES),
        grid=(tokens // seq,),
        in_specs=[pl.BlockSpec(memory_space=pltpu.SMEM), seq_spec, seq_spec, seq_spec],
        out_specs=seq_spec,
        out_shape=jax.ShapeDtypeStruct((tokens, mix_w), BF16),
        compiler_params=_cparams("parallel"),
        name="swa_attn",
    )(sink, q, k, v)


def _merge_kernel(x_ref, g_ref, wg_ref, bg_ref, ya_ref, yb_ref, yc_ref, yd_ref, wb_ref, wo_ref, o_ref):
    x = x_ref[...]
    d_model = x.shape[1]
    hb = _rmsnorm_rows(x, g_ref[...]).astype(BF16)
    mixed = jnp.zeros(x.shape, F32)
    for n, y_ref in enumerate((ya_ref, yb_ref, yc_ref, yd_ref)):
        cols = slice(n * d_model, (n + 1) * d_model)
        gate = jax.nn.sigmoid(jnp.dot(hb, wg_ref[:, cols], preferred_element_type=F32) + bg_ref[:, cols])
        mixed = mixed + gate * jnp.dot(y_ref[...], wb_ref[n], preferred_element_type=F32)
    o_ref[...] = x + jnp.dot(mixed.astype(BF16), wo_ref[...], preferred_element_type=F32)


def _merge(x2, g, w_gate, b_gate, ys, w_branch, w_out, *, tm):
    tokens, d_model = x2.shape
    mix_w = ys[0].shape[1]
    tok_spec = lambda w: pl.BlockSpec((tm, w), lambda i: (i, 0))
    full = lambda shape: pl.BlockSpec(shape, lambda i: (0,) * len(shape))
    return pl.pallas_call(
        _merge_kernel,
        grid=(tokens // tm,),
        in_specs=[tok_spec(d_model), full(g.shape), full(w_gate.shape), full(b_gate.shape)]
                 + [tok_spec(mix_w)] * N_BRANCH
                 + [full(w_branch.shape), full(w_out.shape)],
        out_specs=tok_spec(d_model),
        out_shape=jax.ShapeDtypeStruct((tokens, d_model), F32),
        compiler_params=_cparams("parallel"),
        name="merge",
    )(x2, g, w_gate, b_gate, *ys, w_branch, w_out)


FFN_CHUNK = 256


def _ffn_kernel(x_ref, xp_ref, xn_ref, g_ref, wu_ref, cw_ref, cb_ref, wd_ref, o_ref, *, d_ff):
    j = pl.program_id(1)
    tm = x_ref.shape[0]
    x = x_ref[...]
    g = g_ref[...]
    hp = jnp.where(j > 0, _rmsnorm_rows(xp_ref[...], g), 0.0)
    hn = jnp.where(j < pl.num_programs(1) - 1, _rmsnorm_rows(xn_ref[...], g), 0.0)
    hcat = jnp.concatenate([hp, _rmsnorm_rows(x, g), hn], axis=0).astype(BF16)

    def conv(u, cols):
        lo, mid, hi = (u[SUBLANES - 1:SUBLANES - 1 + tm], u[SUBLANES:SUBLANES + tm],
                       u[SUBLANES + 1:SUBLANES + 1 + tm])
        return (lo * cw_ref[0:1, cols] + mid * cw_ref[1:2, cols] + hi * cw_ref[2:3, cols]
                + cb_ref[:, cols])

    acc = jnp.zeros(x.shape, F32)
    for c in range(d_ff // FFN_CHUNK):
        gcols = slice(c * FFN_CHUNK, (c + 1) * FFN_CHUNK)
        ucols = slice(d_ff + c * FFN_CHUNK, d_ff + (c + 1) * FFN_CHUNK)
        gate = conv(jnp.dot(hcat, wu_ref[:, gcols], preferred_element_type=F32), gcols)
        up = conv(jnp.dot(hcat, wu_ref[:, ucols], preferred_element_type=F32), ucols)
        act = (gate * jax.nn.sigmoid(gate) * up).astype(BF16)
        acc = acc + jnp.dot(act, wd_ref[gcols, :], preferred_element_type=F32)
    o_ref[...] = x + acc


def _ffn(x2, g, w_up, conv_w, conv_b, w_down, *, seq, tm):
    tokens, d_model = x2.shape
    d_ff = w_down.shape[0]
    seq_tiles = seq // tm
    halo_per_tile = tm // SUBLANES
    n_halo = tokens // SUBLANES
    full = lambda shape: pl.BlockSpec(shape, lambda b, j: (0,) * len(shape))
    return pl.pallas_call(
        functools.partial(_ffn_kernel, d_ff=d_ff),
        grid=(tokens // seq, seq_tiles),
        in_specs=[
            pl.BlockSpec((tm, d_model), lambda b, j: (b * seq_tiles + j, 0)),
            pl.BlockSpec((SUBLANES, d_model),
                         lambda b, j: (jnp.maximum((b * seq_tiles + j) * halo_per_tile - 1, 0), 0)),
            pl.BlockSpec((SUBLANES, d_model),
                         lambda b, j: (jnp.minimum((b * seq_tiles + j + 1) * halo_per_tile, n_halo - 1), 0)),
            full(g.shape), full(w_up.shape), full(conv_w.shape), full(conv_b.shape), full(w_down.shape),
        ],
        out_specs=pl.BlockSpec((tm, d_model), lambda b, j: (b * seq_tiles + j, 0)),
        out_shape=jax.ShapeDtypeStruct((tokens, d_model), F32),
        compiler_params=_cparams("parallel", "parallel"),
        name="conv_ffn",
    )(x2, x2, x2, g, w_up, conv_w, conv_b, w_down)


def _rope_lane_tables(seq):
    half = ROPE_DIMS // 2
    pos = jnp.arange(seq, dtype=F32)
    inv = ROPE_THETA ** (-jnp.arange(0, ROPE_DIMS, 2, dtype=F32) / ROPE_DIMS)
    ang = pos[:, None] * inv[None, :]
    cos, sin = jnp.cos(ang), jnp.sin(ang)
    d = np.arange(LANES) % HEAD_DIM
    first, second = d < half, (d >= half) & (d < ROPE_DIMS)
    idx = d % half
    c_tab = jnp.where((first | second)[None, :], cos[:, idx], 1.0)
    a_tab = jnp.where(first[None, :], -sin[:, idx], 0.0)
    b_tab = jnp.where(second[None, :], sin[:, idx], 0.0)
    return jnp.stack([c_tab, a_tab, b_tab]).astype(F32)


def kernel(x, g_mix, w_in, gate_b, a_conv_w, a_conv_b, a_ln_g, a_ln_b, na_qn, na_kn, na_rpb, dil_qn, dil_kn, swa_qn, swa_kn, swa_sink, w_branch, w_out, g_ffn, w_up, ffn_conv_w, ffn_conv_b, w_down):
    bsz, seq, d_model = x.shape
    depth = w_in.shape[0]
    mix_w = d_model // N_BRANCH
    heads = mix_w // HEAD_DIM
    kv_heads = swa_sink.shape[1] // 2
    grp = heads // kv_heads
    n_gate = N_BRANCH * d_model
    assert mix_w % LANES == 0 and seq % (GRID_W * NA_ROWS) == 0
    assert all(w // (2 * d) == DIL_PAIRS[0][0] // 2 for w, d in DIL_PAIRS)

    rope_tab = _rope_lane_tables(seq)
    scale = HEAD_DIM ** -0.5
    tile_h = lambda v: jnp.tile(v.astype(F32), heads)
    x2 = x.reshape(bsz * seq, d_model)

    for l in range(depth):
        w_l = w_in[l]
        o_swk = w_l.shape[1] - 2 * kv_heads * HEAD_DIM
        rep_heads = lambda w: jnp.repeat(w.reshape(d_model, kv_heads, HEAD_DIM), grp, axis=1).reshape(d_model, mix_w)
        w_br = jnp.concatenate([
            w_l[:, n_gate:o_swk],
            rep_heads(w_l[:, o_swk:o_swk + kv_heads * HEAD_DIM]),
            rep_heads(w_l[:, o_swk + kv_heads * HEAD_DIM:]),
        ], axis=1).astype(BF16)
        gains = jnp.stack([tile_h(na_qn[l]) * scale, tile_h(na_kn[l]),
                           tile_h(dil_qn[l]) * scale, tile_h(dil_kn[l]),
                           tile_h(swa_qn[l]) * scale, tile_h(swa_kn[l]),
                           jnp.zeros((mix_w,), F32), jnp.zeros((mix_w,), F32)])
        na_bias = _na_bias_table(na_rpb[l], seq // GRID_W)

        a_u, na_q, na_k, na_v, dil_qkv, sw_q, sw_k, sw_v = _inproj(
            x2, g_mix[l][None, :], w_br, gains, rope_tab, seq=seq, tm=512)
        y_a = _confconv(a_u, a_conv_w[l], a_conv_b[l][None, :], a_ln_g[l][None, :], a_ln_b[l][None, :], seq=seq)
        y_b = _na_attn(na_q, na_k, na_v, na_bias, seq=seq)
        y_c = _dilated_attn(dil_qkv, seq=seq, mix_w=mix_w)
        y_d = _swa_attn(swa_sink[l].astype(F32), sw_q, sw_k, sw_v, seq=seq)
        x2 = _merge(x2, g_mix[l][None, :], w_l[:, :n_gate].astype(BF16), gate_b[l].reshape(1, n_gate),
                    (y_a, y_b, y_c, y_d), w_branch[l].astype(BF16), w_out[l].astype(BF16), tm=512)
        x2 = _ffn(x2, g_ffn[l][None, :], w_up[l].astype(BF16), ffn_conv_w[l], ffn_conv_b[l][None, :],
                  w_down[l].astype(BF16), seq=seq, tm=512)
    return x2.reshape(bsz, seq, d_model)
```

```python
import functools

import numpy as np
import jax
import jax.numpy as jnp
from jax import lax
from jax.experimental import pallas as pl
from jax.experimental.pallas import tpu as pltpu

F32 = jnp.float32
BF16 = jnp.bfloat16

N_BRANCH = 4
HEAD_DIM = 64
CONV_A_K = 31
NA_ROWS = 8
NA_COLS = 16
GRID_W = 64
DIL_PAIRS = ((128, 1), (512, 4), (2048, 16))
SWA_WINDOW = 128
Q_BLOCK = 128
ROPE_THETA = 500000.0
ROPE_DIMS = HEAD_DIM // 4
FFN_CONV_K = 3
EPS = 1e-6
NEG_INF = -1e30

LANES = 128
SUBLANES = 8
VMEM_LIMIT = 56 * 1024 * 1024

HEADS_PER_SLAB = LANES // HEAD_DIM


def _cparams(*sem):
    return pltpu.CompilerParams(dimension_semantics=sem, vmem_limit_bytes=VMEM_LIMIT)


def _low_head_mask():
    return lax.broadcasted_iota(jnp.int32, (1, LANES), 1) < HEAD_DIM


def _rmsnorm_rows(x, g):
    return x * lax.rsqrt(jnp.mean(x * x, axis=-1, keepdims=True) + EPS) * g


def _head_rmsnorm(slab, gain, blockdiag):
    sq = slab * slab
    hi = sq.astype(BF16)
    lo = (sq - hi.astype(F32)).astype(BF16)
    ss = (jnp.dot(hi, blockdiag, preferred_element_type=F32)
          + jnp.dot(lo, blockdiag, preferred_element_type=F32))
    return slab * lax.rsqrt(ss * (1.0 / HEAD_DIM) + EPS) * gain


def _rope(y, rope_ref):
    half = ROPE_DIMS // 2
    return (y * rope_ref[0]
            + pltpu.roll(y, LANES - half, 1) * rope_ref[1]
            + pltpu.roll(y, half, 1) * rope_ref[2])


def _inproj_kernel(x_ref, g_ref, w_ref, gains_ref, rope_ref,
                   a_ref, naq_ref, nak_ref, nav_ref, dil_ref, swq_ref, swk_ref, swv_ref, *, mix_w):
    hb = _rmsnorm_rows(x_ref[...], g_ref[...]).astype(BF16)
    r = lax.broadcasted_iota(jnp.int32, (LANES, LANES), 0) // HEAD_DIM
    c = lax.broadcasted_iota(jnp.int32, (LANES, LANES), 1) // HEAD_DIM
    blockdiag = (r == c).astype(BF16)

    def proj(seg):
        return jnp.dot(hb, w_ref[:, seg * mix_w:(seg + 1) * mix_w], preferred_element_type=F32)

    def slabs(v):
        return [v[:, s * LANES:(s + 1) * LANES] for s in range(mix_w // LANES)]

    def gain(row, s):
        return gains_ref[row:row + 1, s * LANES:(s + 1) * LANES]

    a_ref[...] = proj(0) * jax.nn.sigmoid(proj(1))
    for s, v in enumerate(slabs(proj(2))):
        naq_ref[:, s * LANES:(s + 1) * LANES] = _head_rmsnorm(v, gain(0, s), blockdiag).astype(BF16)
    for s, v in enumerate(slabs(proj(3))):
        nak_ref[:, s * LANES:(s + 1) * LANES] = _head_rmsnorm(v, gain(1, s), blockdiag).astype(BF16)
    nav_ref[...] = proj(4).astype(BF16)
    n_slab = mix_w // LANES
    for s, v in enumerate(slabs(proj(5))):
        dil_ref[s] = _rope(_head_rmsnorm(v, gain(2, s), blockdiag), rope_ref)
    for s, v in enumerate(slabs(proj(6))):
        dil_ref[n_slab + s] = _rope(_head_rmsnorm(v, gain(3, s), blockdiag), rope_ref)
    for s, v in enumerate(slabs(proj(7))):
        dil_ref[2 * n_slab + s] = v
    for s, v in enumerate(slabs(proj(8))):
        swq_ref[:, s * LANES:(s + 1) * LANES] = _rope(_head_rmsnorm(v, gain(4, s), blockdiag), rope_ref).astype(BF16)
    for s, v in enumerate(slabs(proj(9))):
        swk_ref[:, s * LANES:(s + 1) * LANES] = _rope(_head_rmsnorm(v, gain(5, s), blockdiag), rope_ref).astype(BF16)
    swv_ref[...] = proj(10).astype(BF16)


def _inproj(x2, g, w_br, gains, rope_tab, *, seq, tm):
    tokens, d_model = x2.shape
    n_br = w_br.shape[1]
    mix_w = gains.shape[1]
    n_slab = mix_w // LANES
    seq_tiles = seq // tm
    tok_spec = lambda w: pl.BlockSpec((tm, w), lambda i: (i, 0))
    full = lambda shape: pl.BlockSpec(shape, lambda i: (0,) * len(shape))
    out_shape = (
        jax.ShapeDtypeStruct((tokens, mix_w), F32),
        jax.ShapeDtypeStruct((tokens, mix_w), BF16),
        jax.ShapeDtypeStruct((tokens, mix_w), BF16),
        jax.ShapeDtypeStruct((tokens, mix_w), BF16),
        jax.ShapeDtypeStruct((3 * n_slab, tokens, LANES), F32),
        jax.ShapeDtypeStruct((tokens, mix_w), BF16),
        jax.ShapeDtypeStruct((tokens, mix_w), BF16),
        jax.ShapeDtypeStruct((tokens, mix_w), BF16),
    )
    out_specs = (
        tok_spec(mix_w), tok_spec(mix_w), tok_spec(mix_w), tok_spec(mix_w),
        pl.BlockSpec((3 * n_slab, tm, LANES), lambda i: (0, i, 0)),
        tok_spec(mix_w), tok_spec(mix_w), tok_spec(mix_w),
    )
    return pl.pallas_call(
        functools.partial(_inproj_kernel, mix_w=mix_w),
        grid=(tokens // tm,),
        in_specs=[
            tok_spec(d_model),
            full((1, d_model)),
            full((d_model, n_br)),
            full(gains.shape),
            pl.BlockSpec((3, tm, LANES), lambda i: (0, i % seq_tiles, 0)),
        ],
        out_specs=out_specs,
        out_shape=out_shape,
        compiler_params=_cparams("parallel"),
        name="inproj",
    )(x2, g, w_br, gains, rope_tab)


CONV_CHUNK = 64
CONV_PAD = 16


def _confconv_kernel(u_ref, w_ref, b_ref, g_ref, beta_ref, o_ref, pad_ref):
    seq, ch = u_ref.shape
    half = (CONV_A_K - 1) // 2
    zeros = jnp.zeros((CONV_PAD, ch), F32)
    pad_ref[0:CONV_PAD, :] = zeros
    pad_ref[CONV_PAD + seq:2 * CONV_PAD + seq, :] = zeros
    pad_ref[CONV_PAD:CONV_PAD + seq, :] = u_ref[...]

    lead = CONV_PAD - half
    n_shift = (lead + CONV_A_K + SUBLANES - 1) // SUBLANES

    def chunk(ci, carry):
        base = pl.multiple_of(ci * CONV_CHUNK, CONV_CHUNK)
        win = pad_ref[pl.ds(base, CONV_CHUNK + n_shift * SUBLANES), :]
        acc = jnp.zeros((CONV_CHUNK, ch), F32)
        for b in range(SUBLANES):
            z = None
            for a in range(n_shift):
                k = a * SUBLANES + b - lead
                if 0 <= k < CONV_A_K:
                    term = win[a * SUBLANES:(a + 1) * SUBLANES + CONV_CHUNK] * w_ref[k:k + 1, :]
                    z = term if z is None else z + term
            acc = acc + z[b:b + CONV_CHUNK]
        acc = acc + b_ref[...]
        mu = jnp.mean(acc, axis=-1, keepdims=True)
        d = acc - mu
        var = jnp.mean(d * d, axis=-1, keepdims=True)
        y = d * lax.rsqrt(var + EPS) * g_ref[...] + beta_ref[...]
        o_ref[pl.ds(base, CONV_CHUNK), :] = (y * jax.nn.sigmoid(y)).astype(o_ref.dtype)
        return carry

    lax.fori_loop(0, seq // CONV_CHUNK, chunk, 0)


def _confconv(u, w, b, g, beta, *, seq):
    tokens, ch = u.shape
    full = lambda shape: pl.BlockSpec(shape, lambda i: (0,) * len(shape))
    return pl.pallas_call(
        _confconv_kernel,
        grid=(tokens // seq,),
        in_specs=[pl.BlockSpec((seq, ch), lambda i: (i, 0)),
                  full(w.shape), full(b.shape), full(g.shape), full(beta.shape)],
        out_specs=pl.BlockSpec((seq, ch), lambda i: (i, 0)),
        out_shape=jax.ShapeDtypeStruct((tokens, ch), BF16),
        scratch_shapes=[pltpu.VMEM((seq + 2 * CONV_PAD, ch), F32)],
        compiler_params=_cparams("parallel"),
        name="confconv",
    )(u, w, b, g, beta)


ATTN_UNROLL = 4


def _stack_heads(q_slab, low):
    zero = jnp.zeros_like(q_slab)
    return jnp.concatenate([jnp.where(low, q_slab, zero), jnp.where(low, zero, q_slab)], axis=0)


def _unstack_heads(o, low):
    m = o.shape[0] // HEADS_PER_SLAB
    return jnp.where(low, o[:m], o[m:])


def _qk(q_stack, k_slab):
    return lax.dot_general(q_stack, k_slab, (((1,), (1,)), ((), ())), preferred_element_type=F32)


def _band_offsets(nq, nk):
    qi = lax.broadcasted_iota(jnp.int32, (HEADS_PER_SLAB * nq, nk), 0) % nq
    return lax.broadcasted_iota(jnp.int32, (HEADS_PER_SLAB * nq, nk), 1) - qi


def _band_valid(offsets, q0, k0, side):
    shift = q0 - k0
    return (offsets >= shift - side) & (offsets <= shift + side)


def _na_kernel(q_ref, k_ref, v_ref, bias_ref, o_ref, *, rows, n_slab):
    low = _low_head_mask()
    kr = NA_ROWS
    n_edge_lo = kr // 2 + 1
    mid_hi = rows - (kr - kr // 2)

    def row_group(g, carry):
        chains = []
        for i in range(ATTN_UNROLL):
            r = g * ATTN_UNROLL + i
            row_start = jnp.clip(r - kr // 2, 0, rows - kr)
            cls = jnp.where(r < n_edge_lo, r, jnp.where(r <= mid_hi, n_edge_lo, r - mid_hi + n_edge_lo))
            q0 = pl.multiple_of(r * GRID_W, GRID_W)
            k0 = pl.multiple_of(row_start * GRID_W, GRID_W)
            for s in range(n_slab):
                chains.append((cls, s, pl.ds(q0, GRID_W), pl.ds(k0, kr * GRID_W),
                               slice(s * LANES, (s + 1) * LANES)))
        scores = [_qk(_stack_heads(q_ref[qrows, lanes], low), k_ref[krows, lanes]) + bias_ref[cls, s]
                  for cls, s, qrows, krows, lanes in chains]
        probs, dens = [], []
        for sc in scores:
            p = jnp.exp(sc - jnp.max(sc, axis=-1, keepdims=True))
            dens.append(jnp.sum(p, axis=-1, keepdims=True))
            probs.append(p.astype(BF16))
        for (cls, s, qrows, krows, lanes), p, l in zip(chains, probs, dens):
            pv = jnp.dot(p, v_ref[krows, lanes], preferred_element_type=F32)
            o_ref[qrows, lanes] = _unstack_heads(pv / l, low).astype(o_ref.dtype)
        return carry

    lax.fori_loop(0, rows // ATTN_UNROLL, row_group, 0)


def _na_bias_table(rpb, rows):
    kr, kc = NA_ROWS, NA_COLS
    depth, heads = rpb.shape[:2]
    n_edge_lo = kr // 2 + 1
    mid_hi = rows - (kr - kr // 2)
    rep = np.asarray(list(range(n_edge_lo)) + [n_edge_lo] + list(range(mid_hi + 1, rows)))
    row_start = np.clip(rep - kr // 2, 0, rows - kr)
    dr = row_start[:, None] + np.arange(kr)[None, :] - rep[:, None] + (NA_ROWS - 1)
    col = np.arange(GRID_W)
    col_start = np.clip(col - kc // 2, 0, GRID_W - kc)
    valid = (col[None, :] >= col_start[:, None]) & (col[None, :] < col_start[:, None] + kc)
    dc = col[None, :] - col[:, None] + (NA_COLS - 1)
    sel_r = (dr[:, :, None] == np.arange(2 * NA_ROWS - 1)).astype(np.float32)
    sel_c = ((dc[:, :, None] == np.arange(2 * NA_COLS - 1)) & valid[:, :, None]).astype(np.float32)
    tab = jnp.einsum('kir,lhrj,wcj->lkhwic', sel_r, rpb.astype(F32), sel_c,
                     precision=lax.Precision.HIGHEST)
    tab = tab + jnp.where(valid, 0.0, NEG_INF).astype(F32)[None, None, None, :, None, :]
    return tab.reshape(depth, len(rep), heads // HEADS_PER_SLAB, HEADS_PER_SLAB * GRID_W, kr * GRID_W)


def _na_attn(q, k, v, bias, *, seq):
    tokens, mix_w = q.shape
    seq_spec = pl.BlockSpec((seq, mix_w), lambda i: (i, 0))
    return pl.pallas_call(
        functools.partial(_na_kernel, rows=seq // GRID_W, n_slab=mix_w // LANES),
        grid=(tokens // seq,),
        in_specs=[seq_spec, seq_spec, seq_spec,
                  pl.BlockSpec(bias.shape, lambda i: (0, 0, 0, 0))],
        out_specs=seq_spec,
        out_shape=jax.ShapeDtypeStruct((tokens, mix_w), BF16),
        compiler_params=_cparams("parallel"),
        name="na_attn",
    )(q, k, v, bias)


def _dil_group_blocks(chains, low):
    scores = [jnp.where(valid, _qk(_stack_heads(lq().astype(BF16), low), lk().astype(BF16)), NEG_INF)
              for lq, lk, _, valid in chains]
    stats = []
    for sc in scores:
        m = jnp.max(sc, axis=-1, keepdims=True)
        p = jnp.exp(sc - m)
        stats.append((p.astype(BF16), m, jnp.sum(p, axis=-1, keepdims=True)))
    outs = []
    for (_, _, lv, _), (p, m, l) in zip(chains, stats):
        acc = jnp.dot(p, lv().astype(BF16), preferred_element_type=F32)
        nq = acc.shape[0] // HEADS_PER_SLAB
        outs.append((_unstack_heads(acc, low), jnp.where(low, m[:nq], m[nq:]), jnp.where(low, l[:nq], l[nq:])))
    return outs


def _dilated_kernel(qkv_ref, o_ref, acc_ref, m_ref, l_ref, *, n_slab):
    seq = qkv_ref.shape[1]
    low = _low_head_mask()
    (_, dil1), (_, dil2), (_, dil3) = DIL_PAIRS
    side = DIL_PAIRS[0][0] // (2 * dil1)
    qb = Q_BLOCK
    len2, len3 = seq // dil2, seq // dil3
    nblk2 = len2 // qb
    off_sq = _band_offsets(len3, len3)
    off_wide = _band_offsets(qb, 2 * qb)

    def chain(s, qrows, krows, valid):
        load = lambda kind, rows: (lambda: qkv_ref[kind * n_slab + s, rows, :])
        return (load(0, qrows), load(1, krows), load(2, krows), valid)

    def g3(g, carry):
        valid = _band_valid(off_sq, 0, 0, side)
        blocks = [(s, pl.ds(g * ATTN_UNROLL + i, len3, stride=dil3))
                  for i in range(ATTN_UNROLL) for s in range(n_slab)]
        outs = _dil_group_blocks([chain(s, rows, rows, valid) for s, rows in blocks], low)
        for (s, rows), (acc, m, l) in zip(blocks, outs):
            acc_ref[n_slab + s, rows, :], m_ref[n_slab + s, rows, :], l_ref[n_slab + s, rows, :] = acc, m, l
        return carry

    lax.fori_loop(0, dil3 // ATTN_UNROLL, g3, 0)

    def g2(g, carry):
        blocks, chains = [], []
        for i in range(ATTN_UNROLL):
            t = g * ATTN_UNROLL + i
            b = t // nblk2
            a0 = (t % nblk2) * qb
            k0 = jnp.clip(a0 - side, 0, len2 - 2 * qb)
            qrows = pl.ds(b + dil2 * a0, qb, stride=dil2)
            krows = pl.ds(b + dil2 * k0, 2 * qb, stride=dil2)
            valid = _band_valid(off_wide, a0, k0, side)
            for s in range(n_slab):
                blocks.append((s, qrows))
                chains.append(chain(s, qrows, krows, valid))
        for (s, qrows), (acc, m, l) in zip(blocks, _dil_group_blocks(chains, low)):
            acc_ref[s, qrows, :], m_ref[s, qrows, :], l_ref[s, qrows, :] = acc, m, l
        return carry

    lax.fori_loop(0, dil2 * nblk2 // ATTN_UNROLL, g2, 0)

    def g1(g, carry):
        blocks, chains = [], []
        for i in range(ATTN_UNROLL):
            a0 = pl.multiple_of((g * ATTN_UNROLL + i) * qb, qb)
            k0 = pl.multiple_of(jnp.clip(a0 - side, 0, seq - 2 * qb), side)
            valid = _band_valid(off_wide, a0, k0, side)
            for s in range(n_slab):
                blocks.append((s, pl.ds(a0, qb)))
                chains.append(chain(s, pl.ds(a0, qb), pl.ds(k0, 2 * qb), valid))
        for (s, qrows), (acc1, m1, l1) in zip(blocks, _dil_group_blocks(chains, low)):
            m2, m3 = m_ref[s, qrows, :], m_ref[n_slab + s, qrows, :]
            m = jnp.maximum(jnp.maximum(m1, m2), m3)
            w1, w2, w3 = jnp.exp(m1 - m), jnp.exp(m2 - m), jnp.exp(m3 - m)
            num = w1 * acc1 + w2 * acc_ref[s, qrows, :] + w3 * acc_ref[n_slab + s, qrows, :]
            den = w1 * l1 + w2 * l_ref[s, qrows, :] + w3 * l_ref[n_slab + s, qrows, :]
            o_ref[qrows, s * LANES:(s + 1) * LANES] = (num / den).astype(o_ref.dtype)
        return carry

    lax.fori_loop(0, seq // qb // ATTN_UNROLL, g1, 0)


def _dilated_attn(qkv, *, seq, mix_w):
    n3, tokens, _ = qkv.shape
    n_slab = n3 // 3
    return pl.pallas_call(
        functools.partial(_dilated_kernel, n_slab=n_slab),
        grid=(tokens // seq,),
        in_specs=[pl.BlockSpec((n3, seq, LANES), lambda i: (0, i, 0))],
        out_specs=pl.BlockSpec((seq, mix_w), lambda i: (i, 0)),
        out_shape=jax.ShapeDtypeStruct((tokens, mix_w), BF16),
        scratch_shapes=[pltpu.VMEM((2 * n_slab, seq, LANES), F32)] * 3,
        compiler_params=_cparams("parallel"),
        name="dilated_attn",
    )(qkv)


def _swa_kernel(sink_ref, q_ref, k_ref, v_ref, o_ref, *, n_slab):
    seq = q_ref.shape[0]
    low = _low_head_mask()
    qb = Q_BLOCK
    nk = qb + 2 * SWA_WINDOW
    offsets = _band_offsets(qb, nk)
    first_head = lax.broadcasted_iota(jnp.int32, (HEADS_PER_SLAB * qb, 1), 0) < qb

    def block_group(g, carry):
        chains = []
        for i in range(ATTN_UNROLL):
            a0 = pl.multiple_of((g * ATTN_UNROLL + i) * qb, qb)
            k0 = pl.multiple_of(jnp.clip(a0 - SWA_WINDOW, 0, seq - nk), qb)
            valid = _band_valid(offsets, a0, k0, SWA_WINDOW)
            for s in range(n_slab):
                chains.append((s, pl.ds(a0, qb), pl.ds(k0, nk), slice(s * LANES, (s + 1) * LANES), valid))
        scores = [jnp.where(valid, _qk(_stack_heads(q_ref[qrows, lanes], low), k_ref[krows, lanes]), NEG_INF)
                  for s, qrows, krows, lanes, valid in chains]
        probs, dens = [], []
        for (s, _, _, _, _), sc in zip(chains, scores):
            sink = jnp.where(first_head, sink_ref[s * HEADS_PER_SLAB], sink_ref[s * HEADS_PER_SLAB + 1])
            m = jnp.maximum(jnp.max(sc, axis=-1, keepdims=True), sink)
            p = jnp.exp(sc - m)
            dens.append(jnp.sum(p, axis=-1, keepdims=True) + jnp.exp(sink - m))
            probs.append(p.astype(BF16))
        for (s, qrows, krows, lanes, _), p, denom in zip(chains, probs, dens):
            pv = jnp.dot(p, v_ref[krows, lanes], preferred_element_type=F32)
            o_ref[qrows, lanes] = _unstack_heads(pv / denom, low).astype(o_ref.dtype)
        return carry

    lax.fori_loop(0, seq // qb // ATTN_UNROLL, block_group, 0)


def _swa_attn(sink, q, k, v, *, seq):
    tokens, mix_w = q.shape
    seq_spec = pl.BlockSpec((seq, mix_w), lambda i: (i, 0))
    return pl.pallas_call(
        functools.partial(_swa_kernel, n_slab=mix_w // LANES),
        grid=(tokens // seq,),
        in_specs=[pl.BlockSpec(memory_space=pltpu.SMEM), seq_spec, seq_spec, seq_spec],
        out_specs=seq_spec,
        out_shape=jax.ShapeDtypeStruct((tokens, mix_w), BF16),
        compiler_params=_cparams("parallel"),
        name="swa_attn",
    )(sink, q, k, v)


def _merge_kernel(x_ref, g_ref, wg_ref, bg_ref, ya_ref, yb_ref, yc_ref, yd_ref, wb_ref, wo_ref, o_ref):
    x = x_ref[...]
    d_model = x.shape[1]
    hb = _rmsnorm_rows(x, g_ref[...]).astype(BF16)
    mixed = jnp.zeros(x.shape, F32)
    for n, y_ref in enumerate((ya_ref, yb_ref, yc_ref, yd_ref)):
        cols = slice(n * d_model, (n + 1) * d_model)
        gate = jax.nn.sigmoid(jnp.dot(hb, wg_ref[:, cols], preferred_element_type=F32) + bg_ref[:, cols])
        mixed = mixed + gate * jnp.dot(y_ref[...], wb_ref[n], preferred_element_type=F32)
    o_ref[...] = x + jnp.dot(mixed.astype(BF16), wo_ref[...], preferred_element_type=F32)


def _merge(x2, g, w_gate, b_gate, ys, w_branch, w_out, *, tm):
    tokens, d_model = x2.shape
    mix_w = ys[0].shape[1]
    tok_spec = lambda w: pl.BlockSpec((tm, w), lambda i: (i, 0))
    full = lambda shape: pl.BlockSpec(shape, lambda i: (0,) * len(shape))
    return pl.pallas_call(
        _merge_kernel,
        grid=(tokens // tm,),
        in_specs=[tok_spec(d_model), full(g.shape), full(w_gate.shape), full(b_gate.shape)]
                 + [tok_spec(mix_w)] * N_BRANCH
                 + [full(w_branch.shape), full(w_out.shape)],
        out_specs=tok_spec(d_model),
        out_shape=jax.ShapeDtypeStruct((tokens, d_model), F32),
        compiler_params=_cparams("parallel"),
        name="merge",
    )(x2, g, w_gate, b_gate, *ys, w_branch, w_out)


FFN_CHUNK = 256


def _ffn_kernel(x_ref, xp_ref, xn_ref, g_ref, wu_ref, cw_ref, cb_ref, wd_ref, o_ref, *, d_ff):
    j = pl.program_id(1)
    tm = x_ref.shape[0]
    x = x_ref[...]
    g = g_ref[...]
    hp = jnp.where(j > 0, _rmsnorm_rows(xp_ref[...], g), 0.0)
    hn = jnp.where(j < pl.num_programs(1) - 1, _rmsnorm_rows(xn_ref[...], g), 0.0)
    hcat = jnp.concatenate([hp, _rmsnorm_rows(x, g), hn], axis=0).astype(BF16)

    def conv(u, cols):
        lo, mid, hi = (u[SUBLANES - 1:SUBLANES - 1 + tm], u[SUBLANES:SUBLANES + tm],
                       u[SUBLANES + 1:SUBLANES + 1 + tm])
        return (lo * cw_ref[0:1, cols] + mid * cw_ref[1:2, cols] + hi * cw_ref[2:3, cols]
                + cb_ref[:, cols])

    acc = jnp.zeros(x.shape, F32)
    for c in range(d_ff // FFN_CHUNK):
        gcols = slice(c * FFN_CHUNK, (c + 1) * FFN_CHUNK)
        ucols = slice(d_ff + c * FFN_CHUNK, d_ff + (c + 1) * FFN_CHUNK)
        gate = conv(jnp.dot(hcat, wu_ref[:, gcols], preferred_element_type=F32), gcols)
        up = conv(jnp.dot(hcat, wu_ref[:, ucols], preferred_element_type=F32), ucols)
        act = (gate * jax.nn.sigmoid(gate) * up).astype(BF16)
        acc = acc + jnp.dot(act, wd_ref[gcols, :], preferred_element_type=F32)
    o_ref[...] = x + acc


def _ffn(x2, g, w_up, conv_w, conv_b, w_down, *, seq, tm):
    tokens, d_model = x2.shape
    d_ff = w_down.shape[0]
    seq_tiles = seq // tm
    halo_per_tile = tm // SUBLANES
    n_halo = tokens // SUBLANES
    full = lambda shape: pl.BlockSpec(shape, lambda b, j: (0,) * len(shape))
    return pl.pallas_call(
        functools.partial(_ffn_kernel, d_ff=d_ff),
        grid=(tokens // seq, seq_tiles),
        in_specs=[
            pl.BlockSpec((tm, d_model), lambda b, j: (b * seq_tiles + j, 0)),
            pl.BlockSpec((SUBLANES, d_model),
                         lambda b, j: (jnp.maximum((b * seq_tiles + j) * halo_per_tile - 1, 0), 0)),
            pl.BlockSpec((SUBLANES, d_model),
                         lambda b, j: (jnp.minimum((b * seq_tiles + j + 1) * halo_per_tile, n_halo - 1), 0)),
            full(g.shape), full(w_up.shape), full(conv_w.shape), full(conv_b.shape), full(w_down.shape),
        ],
        out_specs=pl.BlockSpec((tm, d_model), lambda b, j: (b * seq_tiles + j, 0)),
        out_shape=jax.ShapeDtypeStruct((tokens, d_model), F32),
        compiler_params=_cparams("parallel", "parallel"),
        name="conv_ffn",
    )(x2, x2, x2, g, w_up, conv_w, conv_b, w_down)


def _rope_lane_tables(seq):
    half = ROPE_DIMS // 2
    pos = jnp.arange(seq, dtype=F32)
    inv = ROPE_THETA ** (-jnp.arange(0, ROPE_DIMS, 2, dtype=F32) / ROPE_DIMS)
    ang = pos[:, None] * inv[None, :]
    cos, sin = jnp.cos(ang), jnp.sin(ang)
    d = np.arange(LANES) % HEAD_DIM
    first, second = d < half, (d >= half) & (d < ROPE_DIMS)
    idx = d % half
    c_tab = jnp.where((first | second)[None, :], cos[:, idx], 1.0)
    a_tab = jnp.where(first[None, :], -sin[:, idx], 0.0)
    b_tab = jnp.where(second[None, :], sin[:, idx], 0.0)
    return jnp.stack([c_tab, a_tab, b_tab]).astype(F32)


def kernel(x, g_mix, w_in, gate_b, a_conv_w, a_conv_b, a_ln_g, a_ln_b, na_qn, na_kn, na_rpb, dil_qn, dil_kn, swa_qn, swa_kn, swa_sink, w_branch, w_out, g_ffn, w_up, ffn_conv_w, ffn_conv_b, w_down):
    bsz, seq, d_model = x.shape
    depth = w_in.shape[0]
    mix_w = d_model // N_BRANCH
    heads = mix_w // HEAD_DIM
    kv_heads = swa_sink.shape[1] // 2
    grp = heads // kv_heads
    n_gate = N_BRANCH * d_model
    assert mix_w % LANES == 0 and seq % (GRID_W * NA_ROWS) == 0
    assert all(w // (2 * d) == DIL_PAIRS[0][0] // 2 for w, d in DIL_PAIRS)

    rope_tab = _rope_lane_tables(seq)
    scale = HEAD_DIM ** -0.5
    tile_h = lambda v: jnp.tile(v.astype(F32), heads)
    x2 = x.reshape(bsz * seq, d_model)
    na_bias = _na_bias_table(na_rpb, seq // GRID_W)

    for l in range(depth):
        w_l = w_in[l]
        o_swk = w_l.shape[1] - 2 * kv_heads * HEAD_DIM
        rep_heads = lambda w: jnp.repeat(w.reshape(d_model, kv_heads, HEAD_DIM), grp, axis=1).reshape(d_model, mix_w)
        w_br = jnp.concatenate([
            w_l[:, n_gate:o_swk],
            rep_heads(w_l[:, o_swk:o_swk + kv_heads * HEAD_DIM]),
            rep_heads(w_l[:, o_swk + kv_heads * HEAD_DIM:]),
        ], axis=1).astype(BF16)
        gains = jnp.stack([tile_h(na_qn[l]) * scale, tile_h(na_kn[l]),
                           tile_h(dil_qn[l]) * scale, tile_h(dil_kn[l]),
                           tile_h(swa_qn[l]) * scale, tile_h(swa_kn[l]),
                           jnp.zeros((mix_w,), F32), jnp.zeros((mix_w,), F32)])

        a_u, na_q, na_k, na_v, dil_qkv, sw_q, sw_k, sw_v = _inproj(
            x2, g_mix[l][None, :], w_br, gains, rope_tab, seq=seq, tm=512)
        y_a = _confconv(a_u, a_conv_w[l], a_conv_b[l][None, :], a_ln_g[l][None, :], a_ln_b[l][None, :], seq=seq)
        y_b = _na_attn(na_q, na_k, na_v, na_bias[l], seq=seq)
        y_c = _dilated_attn(dil_qkv, seq=seq, mix_w=mix_w)
        y_d = _swa_attn(swa_sink[l].astype(F32), sw_q, sw_k, sw_v, seq=seq)
        x2 = _merge(x2, g_mix[l][None, :], w_l[:, :n_gate].astype(BF16), gate_b[l].reshape(1, n_gate),
                    (y_a, y_b, y_c, y_d), w_branch[l].astype(BF16), w_out[l].astype(BF16), tm=512)
        x2 = _ffn(x2, g_ffn[l][None, :], w_up[l].astype(BF16), ffn_conv_w[l], ffn_conv_b[l][None, :],
                  w_down[l].astype(BF16), seq=seq, tm=512)
    return x2.reshape(bsz, seq, d_model)
```

```python
import functools

import numpy as np
import jax
import jax.numpy as jnp
from jax import lax
from jax.experimental import pallas as pl
from jax.experimental.pallas import tpu as pltpu

F32 = jnp.float32
BF16 = jnp.bfloat16

N_BRANCH = 4
HEAD_DIM = 64
CONV_A_K = 31
NA_ROWS = 8
NA_COLS = 16
GRID_W = 64
DIL_PAIRS = ((128, 1), (512, 4), (2048, 16))
SWA_WINDOW = 128
Q_BLOCK = 128
ROPE_THETA = 500000.0
ROPE_DIMS = HEAD_DIM // 4
FFN_CONV_K = 3
EPS = 1e-6
NEG_INF = -1e30

LANES = 128
SUBLANES = 8
VMEM_LIMIT = 56 * 1024 * 1024

HEADS_PER_SLAB = LANES // HEAD_DIM


def _cparams(*sem):
    return pltpu.CompilerParams(dimension_semantics=sem, vmem_limit_bytes=VMEM_LIMIT)


def _low_head_mask():
    return lax.broadcasted_iota(jnp.int32, (1, LANES), 1) < HEAD_DIM


def _rmsnorm_rows(x, g):
    return x * lax.rsqrt(jnp.mean(x * x, axis=-1, keepdims=True) + EPS) * g


def _rope(y, rope_ref):
    half = ROPE_DIMS // 2
    return (y * rope_ref[0]
            + pltpu.roll(y, LANES - half, 1) * rope_ref[1]
            + pltpu.roll(y, half, 1) * rope_ref[2])


(SEG_GLU_A, SEG_GLU_G, SEG_NA_Q, SEG_NA_K, SEG_NA_V, SEG_DIL_Q, SEG_DIL_K, SEG_DIL_V,
 SEG_SW_Q, SEG_SW_K, SEG_SW_V) = range(11)
NORMED_SEGS = (SEG_NA_Q, SEG_NA_K, SEG_DIL_Q, SEG_DIL_K, SEG_SW_Q, SEG_SW_K)


def _inproj_kernel(x_ref, g_ref, w_ref, gains_ref, rope_ref,
                   a_ref, naq_ref, nak_ref, nav_ref, dil_ref, swq_ref, swk_ref, swv_ref, *, mix_w):
    n_slab = mix_w // LANES
    hb = _rmsnorm_rows(x_ref[...], g_ref[...]).astype(BF16)
    r = lax.broadcasted_iota(jnp.int32, (mix_w, mix_w), 0) // HEAD_DIM
    c = lax.broadcasted_iota(jnp.int32, (mix_w, mix_w), 1) // HEAD_DIM
    blockdiag = (r == c).astype(BF16)

    def proj(seg):
        return jnp.dot(hb, w_ref[:, seg * mix_w:(seg + 1) * mix_w], preferred_element_type=F32)

    def head_norm(row, v):
        ms = jnp.dot((v * v).astype(BF16), blockdiag, preferred_element_type=F32) * (1.0 / HEAD_DIM)
        return v * lax.rsqrt(ms + EPS) * gains_ref[row:row + 1, :]

    p, normed = {}, {}
    for row, seg in enumerate(NORMED_SEGS):
        p[seg] = proj(seg)
        if row:
            normed[NORMED_SEGS[row - 1]] = head_norm(row - 1, p[NORMED_SEGS[row - 1]])
    for seg in range(SEG_SW_V + 1):
        if seg not in p:
            p[seg] = proj(seg)
            if NORMED_SEGS[-1] not in normed:
                normed[NORMED_SEGS[-1]] = head_norm(len(NORMED_SEGS) - 1, p[NORMED_SEGS[-1]])

    def slabs(v):
        return [v[:, s * LANES:(s + 1) * LANES] for s in range(n_slab)]

    a_ref[...] = p[SEG_GLU_A] * jax.nn.sigmoid(p[SEG_GLU_G])
    naq_ref[...] = normed[SEG_NA_Q].astype(BF16)
    nak_ref[...] = normed[SEG_NA_K].astype(BF16)
    nav_ref[...] = p[SEG_NA_V].astype(BF16)
    for s in range(n_slab):
        dil_ref[s] = _rope(slabs(normed[SEG_DIL_Q])[s], rope_ref)
        dil_ref[n_slab + s] = _rope(slabs(normed[SEG_DIL_K])[s], rope_ref)
        dil_ref[2 * n_slab + s] = slabs(p[SEG_DIL_V])[s]
        lanes = slice(s * LANES, (s + 1) * LANES)
        swq_ref[:, lanes] = _rope(slabs(normed[SEG_SW_Q])[s], rope_ref).astype(BF16)
        swk_ref[:, lanes] = _rope(slabs(normed[SEG_SW_K])[s], rope_ref).astype(BF16)
    swv_ref[...] = p[SEG_SW_V].astype(BF16)


def _inproj(x2, g, w_br, gains, rope_tab, *, seq, tm):
    tokens, d_model = x2.shape
    n_br = w_br.shape[1]
    mix_w = gains.shape[1]
    n_slab = mix_w // LANES
    seq_tiles = seq // tm
    tok_spec = lambda w: pl.BlockSpec((tm, w), lambda i: (i, 0))
    full = lambda shape: pl.BlockSpec(shape, lambda i: (0,) * len(shape))
    out_shape = (
        jax.ShapeDtypeStruct((tokens, mix_w), F32),
        jax.ShapeDtypeStruct((tokens, mix_w), BF16),
        jax.ShapeDtypeStruct((tokens, mix_w), BF16),
        jax.ShapeDtypeStruct((tokens, mix_w), BF16),
        jax.ShapeDtypeStruct((3 * n_slab, tokens, LANES), F32),
        jax.ShapeDtypeStruct((tokens, mix_w), BF16),
        jax.ShapeDtypeStruct((tokens, mix_w), BF16),
        jax.ShapeDtypeStruct((tokens, mix_w), BF16),
    )
    out_specs = (
        tok_spec(mix_w), tok_spec(mix_w), tok_spec(mix_w), tok_spec(mix_w),
        pl.BlockSpec((3 * n_slab, tm, LANES), lambda i: (0, i, 0)),
        tok_spec(mix_w), tok_spec(mix_w), tok_spec(mix_w),
    )
    return pl.pallas_call(
        functools.partial(_inproj_kernel, mix_w=mix_w),
        grid=(tokens // tm,),
        in_specs=[
            tok_spec(d_model),
            full((1, d_model)),
            full((d_model, n_br)),
            full(gains.shape),
            pl.BlockSpec((3, tm, LANES), lambda i: (0, i % seq_tiles, 0)),
        ],
        out_specs=out_specs,
        out_shape=out_shape,
        compiler_params=_cparams("parallel"),
        name="inproj",
    )(x2, g, w_br, gains, rope_tab)


CONV_CHUNK = 128
CONV_PAD = 16


def _confconv_kernel(u_ref, w_ref, b_ref, g_ref, beta_ref, o_ref, pad_ref):
    seq, ch = u_ref.shape
    n_slab = ch // LANES
    half = (CONV_A_K - 1) // 2
    zeros = jnp.zeros((CONV_PAD, LANES), F32)
    for s in range(n_slab):
        pad_ref[s, 0:CONV_PAD, :] = zeros
        pad_ref[s, CONV_PAD + seq:2 * CONV_PAD + seq, :] = zeros
        pad_ref[s, CONV_PAD:CONV_PAD + seq, :] = u_ref[:, s * LANES:(s + 1) * LANES]

    def chunk(ci, carry):
        base = pl.multiple_of(ci * CONV_CHUNK, CONV_CHUNK)
        accs = []
        for s in range(n_slab):
            lanes = slice(s * LANES, (s + 1) * LANES)
            acc = b_ref[:, lanes]
            for k in range(CONV_A_K):
                rows = pl.ds(base + (CONV_PAD - half + k), CONV_CHUNK, stride=1)
                acc = acc + pad_ref[s, rows, :] * w_ref[k:k + 1, lanes]
            accs.append(acc)
        mu = sum(jnp.sum(a, axis=-1, keepdims=True) for a in accs) * (1.0 / ch)
        ds = [a - mu for a in accs]
        var = sum(jnp.sum(d * d, axis=-1, keepdims=True) for d in ds) * (1.0 / ch)
        inv = lax.rsqrt(var + EPS)
        for s, d in enumerate(ds):
            lanes = slice(s * LANES, (s + 1) * LANES)
            y = d * inv * g_ref[:, lanes] + beta_ref[:, lanes]
            o_ref[pl.ds(base, CONV_CHUNK), lanes] = (y * jax.nn.sigmoid(y)).astype(o_ref.dtype)
        return carry

    lax.fori_loop(0, seq // CONV_CHUNK, chunk, 0)


def _confconv(u, w, b, g, beta, *, seq):
    tokens, ch = u.shape
    full = lambda shape: pl.BlockSpec(shape, lambda i: (0,) * len(shape))
    return pl.pallas_call(
        _confconv_kernel,
        grid=(tokens // seq,),
        in_specs=[pl.BlockSpec((seq, ch), lambda i: (i, 0)),
                  full(w.shape), full(b.shape), full(g.shape), full(beta.shape)],
        out_specs=pl.BlockSpec((seq, ch), lambda i: (i, 0)),
        out_shape=jax.ShapeDtypeStruct((tokens, ch), BF16),
        scratch_shapes=[pltpu.VMEM((ch // LANES, seq + 2 * CONV_PAD, LANES), F32)],
        compiler_params=_cparams("parallel"),
        name="confconv",
    )(u, w, b, g, beta)


ATTN_UNROLL = 4


def _stack_heads(q_slab, low):
    zero = jnp.zeros_like(q_slab)
    return jnp.concatenate([jnp.where(low, q_slab, zero), jnp.where(low, zero, q_slab)], axis=0)


def _unstack_heads(o, low):
    m = o.shape[0] // HEADS_PER_SLAB
    return jnp.where(low, o[:m], o[m:])


def _qk(q_stack, k_slab):
    return lax.dot_general(q_stack, k_slab, (((1,), (1,)), ((), ())), preferred_element_type=F32)


def _band_offsets(nq, nk):
    qi = lax.broadcasted_iota(jnp.int32, (HEADS_PER_SLAB * nq, nk), 0) % nq
    return lax.broadcasted_iota(jnp.int32, (HEADS_PER_SLAB * nq, nk), 1) - qi


def _band_valid(offsets, q0, k0, side):
    shift = q0 - k0
    return (offsets >= shift - side) & (offsets <= shift + side)


def _na_kernel(q_ref, k_ref, v_ref, bias_ref, o_ref, *, rows, n_slab):
    low = _low_head_mask()
    kr = NA_ROWS
    n_edge_lo = kr // 2 + 1
    mid_hi = rows - (kr - kr // 2)

    def row_group(g, carry):
        chains = []
        for i in range(ATTN_UNROLL):
            r = g * ATTN_UNROLL + i
            row_start = jnp.clip(r - kr // 2, 0, rows - kr)
            cls = jnp.where(r < n_edge_lo, r, jnp.where(r <= mid_hi, n_edge_lo, r - mid_hi + n_edge_lo))
            q0 = pl.multiple_of(r * GRID_W, GRID_W)
            k0 = pl.multiple_of(row_start * GRID_W, GRID_W)
            for s in range(n_slab):
                chains.append((cls, s, pl.ds(q0, GRID_W), pl.ds(k0, kr * GRID_W),
                               slice(s * LANES, (s + 1) * LANES)))
        scores = [_qk(_stack_heads(q_ref[qrows, lanes], low), k_ref[krows, lanes]) + bias_ref[cls, s]
                  for cls, s, qrows, krows, lanes in chains]
        probs, dens = [], []
        for sc in scores:
            p = jnp.exp(sc - jnp.max(sc, axis=-1, keepdims=True))
            dens.append(jnp.sum(p, axis=-1, keepdims=True))
            probs.append(p.astype(BF16))
        for (cls, s, qrows, krows, lanes), p, l in zip(chains, probs, dens):
            pv = jnp.dot(p, v_ref[krows, lanes], preferred_element_type=F32)
            o_ref[qrows, lanes] = _unstack_heads(pv / l, low).astype(o_ref.dtype)
        return carry

    lax.fori_loop(0, rows // ATTN_UNROLL, row_group, 0)


def _na_bias_table(rpb, rows):
    kr, kc = NA_ROWS, NA_COLS
    depth, heads = rpb.shape[:2]
    n_edge_lo = kr // 2 + 1
    mid_hi = rows - (kr - kr // 2)
    rep = np.asarray(list(range(n_edge_lo)) + [n_edge_lo] + list(range(mid_hi + 1, rows)))
    row_start = np.clip(rep - kr // 2, 0, rows - kr)
    dr = row_start[:, None] + np.arange(kr)[None, :] - rep[:, None] + (NA_ROWS - 1)
    col = np.arange(GRID_W)
    col_start = np.clip(col - kc // 2, 0, GRID_W - kc)
    valid = (col[None, :] >= col_start[:, None]) & (col[None, :] < col_start[:, None] + kc)
    dc = col[None, :] - col[:, None] + (NA_COLS - 1)
    sel_r = (dr[:, :, None] == np.arange(2 * NA_ROWS - 1)).astype(np.float32)
    sel_c = ((dc[:, :, None] == np.arange(2 * NA_COLS - 1)) & valid[:, :, None]).astype(np.float32)
    tab = jnp.einsum('kir,lhrj,wcj->lkhwic', sel_r, rpb.astype(F32), sel_c,
                     precision=lax.Precision.HIGHEST)
    tab = tab + jnp.where(valid, 0.0, NEG_INF).astype(F32)[None, None, None, :, None, :]
    return tab.reshape(depth, len(rep), heads // HEADS_PER_SLAB, HEADS_PER_SLAB * GRID_W, kr * GRID_W)


def _na_attn(q, k, v, bias, *, seq):
    tokens, mix_w = q.shape
    seq_spec = pl.BlockSpec((seq, mix_w), lambda i: (i, 0))
    return pl.pallas_call(
        functools.partial(_na_kernel, rows=seq // GRID_W, n_slab=mix_w // LANES),
        grid=(tokens // seq,),
        in_specs=[seq_spec, seq_spec, seq_spec,
                  pl.BlockSpec(bias.shape, lambda i: (0, 0, 0, 0))],
        out_specs=seq_spec,
        out_shape=jax.ShapeDtypeStruct((tokens, mix_w), BF16),
        compiler_params=_cparams("parallel"),
        name="na_attn",
    )(q, k, v, bias)


def _dil_group_blocks(chains, low):
    scores = [jnp.where(valid, _qk(_stack_heads(lq().astype(BF16), low), lk().astype(BF16)), NEG_INF)
              for lq, lk, _, valid in chains]
    stats = []
    for sc in scores:
        m = jnp.max(sc, axis=-1, keepdims=True)
        p = jnp.exp(sc - m)
        stats.append((p.astype(BF16), m, jnp.sum(p, axis=-1, keepdims=True)))
    outs = []
    for (_, _, lv, _), (p, m, l) in zip(chains, stats):
        acc = jnp.dot(p, lv().astype(BF16), preferred_element_type=F32)
        nq = acc.shape[0] // HEADS_PER_SLAB
        outs.append((_unstack_heads(acc, low), jnp.where(low, m[:nq], m[nq:]), jnp.where(low, l[:nq], l[nq:])))
    return outs


def _dilated_kernel(qkv_ref, o_ref, acc_ref, m_ref, l_ref, *, n_slab):
    seq = qkv_ref.shape[1]
    low = _low_head_mask()
    (_, dil1), (_, dil2), (_, dil3) = DIL_PAIRS
    side = DIL_PAIRS[0][0] // (2 * dil1)
    qb = Q_BLOCK
    len2, len3 = seq // dil2, seq // dil3
    nblk2 = len2 // qb
    off_sq = _band_offsets(len3, len3)
    off_wide = _band_offsets(qb, 2 * qb)

    def chain(s, qrows, krows, valid):
        load = lambda kind, rows: (lambda: qkv_ref[kind * n_slab + s, rows, :])
        return (load(0, qrows), load(1, krows), load(2, krows), valid)

    def g3(g, carry):
        valid = _band_valid(off_sq, 0, 0, side)
        blocks = [(s, pl.ds(g * ATTN_UNROLL + i, len3, stride=dil3))
                  for i in range(ATTN_UNROLL) for s in range(n_slab)]
        outs = _dil_group_blocks([chain(s, rows, rows, valid) for s, rows in blocks], low)
        for (s, rows), (acc, m, l) in zip(blocks, outs):
            acc_ref[n_slab + s, rows, :], m_ref[n_slab + s, rows, :], l_ref[n_slab + s, rows, :] = acc, m, l
        return carry

    lax.fori_loop(0, dil3 // ATTN_UNROLL, g3, 0)

    def g2(g, carry):
        blocks, chains = [], []
        for i in range(ATTN_UNROLL):
            t = g * ATTN_UNROLL + i
            b = t // nblk2
            a0 = (t % nblk2) * qb
            k0 = jnp.clip(a0 - side, 0, len2 - 2 * qb)
            qrows = pl.ds(b + dil2 * a0, qb, stride=dil2)
            krows = pl.ds(b + dil2 * k0, 2 * qb, stride=dil2)
            valid = _band_valid(off_wide, a0, k0, side)
            for s in range(n_slab):
                blocks.append((s, qrows))
                chains.append(chain(s, qrows, krows, valid))
        for (s, qrows), (acc, m, l) in zip(blocks, _dil_group_blocks(chains, low)):
            acc_ref[s, qrows, :], m_ref[s, qrows, :], l_ref[s, qrows, :] = acc, m, l
        return carry

    lax.fori_loop(0, dil2 * nblk2 // ATTN_UNROLL, g2, 0)

    def g1(g, carry):
        blocks, chains = [], []
        for i in range(ATTN_UNROLL):
            a0 = pl.multiple_of((g * ATTN_UNROLL + i) * qb, qb)
            k0 = pl.multiple_of(jnp.clip(a0 - side, 0, seq - 2 * qb), side)
            valid = _band_valid(off_wide, a0, k0, side)
            for s in range(n_slab):
                blocks.append((s, pl.ds(a0, qb)))
                chains.append(chain(s, pl.ds(a0, qb), pl.ds(k0, 2 * qb), valid))
        for (s, qrows), (acc1, m1, l1) in zip(blocks, _dil_group_blocks(chains, low)):
            m2, m3 = m_ref[s, qrows, :], m_ref[n_slab + s, qrows, :]
            m = jnp.maximum(jnp.maximum(m1, m2), m3)
            w1, w2, w3 = jnp.exp(m1 - m), jnp.exp(m2 - m), jnp.exp(m3 - m)
            num = w1 * acc1 + w2 * acc_ref[s, qrows, :] + w3 * acc_ref[n_slab + s, qrows, :]
            den = w1 * l1 + w2 * l_ref[s, qrows, :] + w3 * l_ref[n_slab + s, qrows, :]
            o_ref[qrows, s * LANES:(s + 1) * LANES] = (num / den).astype(o_ref.dtype)
        return carry

    lax.fori_loop(0, seq // qb // ATTN_UNROLL, g1, 0)


def _dilated_attn(qkv, *, seq, mix_w):
    n3, tokens, _ = qkv.shape
    n_slab = n3 // 3
    return pl.pallas_call(
        functools.partial(_dilated_kernel, n_slab=n_slab),
        grid=(tokens // seq,),
        in_specs=[pl.BlockSpec((n3, seq, LANES), lambda i: (0, i, 0))],
        out_specs=pl.BlockSpec((seq, mix_w), lambda i: (i, 0)),
        out_shape=jax.ShapeDtypeStruct((tokens, mix_w), BF16),
        scratch_shapes=[pltpu.VMEM((2 * n_slab, seq, LANES), F32)] * 3,
        compiler_params=_cparams("parallel"),
        name="dilated_attn",
    )(qkv)


def _swa_kernel(sink_ref, q_ref, k_ref, v_ref, o_ref, *, n_slab):
    seq = q_ref.shape[0]
    low = _low_head_mask()
    qb = Q_BLOCK
    nk = qb + 2 * SWA_WINDOW
    offsets = _band_offsets(qb, nk)
    first_head = lax.broadcasted_iota(jnp.int32, (HEADS_PER_SLAB * qb, 1), 0) < qb

    def block_group(g, carry):
        chains = []
        for i in range(ATTN_UNROLL):
            a0 = pl.multiple_of((g * ATTN_UNROLL + i) * qb, qb)
            k0 = pl.multiple_of(jnp.clip(a0 - SWA_WINDOW, 0, seq - nk), qb)
            valid = _band_valid(offsets, a0, k0, SWA_WINDOW)
            for s in range(n_slab):
                chains.append((s, pl.ds(a0, qb), pl.ds(k0, nk), slice(s * LANES, (s + 1) * LANES), valid))
        scores = [jnp.where(valid, _qk(_stack_heads(q_ref[qrows, lanes], low), k_ref[krows, lanes]), NEG_INF)
                  for s, qrows, krows, lanes, valid in chains]
        probs, dens = [], []
        for (s, _, _, _, _), sc in zip(chains, scores):
            sink = jnp.where(first_head, sink_ref[s * HEADS_PER_SLAB], sink_ref[s * HEADS_PER_SLAB + 1])
            m = jnp.maximum(jnp.max(sc, axis=-1, keepdims=True), sink)
            p = jnp.exp(sc - m)
            dens.append(jnp.sum(p, axis=-1, keepdims=True) + jnp.exp(sink - m))
            probs.append(p.astype(BF16))
        for (s, qrows, krows, lanes, _), p, denom in zip(chains, probs, dens):
            pv = jnp.dot(p, v_ref[krows, lanes], preferred_element_type=F32)
            o_ref[qrows, lanes] = _unstack_heads(pv / denom, low).astype(o_ref.dtype)
        return carry

    lax.fori_loop(0, seq // qb // ATTN_UNROLL, block_group, 0)


def _swa_attn(sink, q, k, v, *, seq):
    tokens, mix_w = q.shape
    seq_spec = pl.BlockSpec((seq, mix_w), lambda i: (i, 0))
    return pl.pallas_call(
        functools.partial(_swa_kernel, n_slab=mix_w // LANES),
        grid=(tokens // seq,),
        in_specs=[pl.BlockSpec(memory_space=pltpu.SMEM), seq_spec, seq_spec, seq_spec],
        out_specs=seq_spec,
        out_shape=jax.ShapeDtypeStruct((tokens, mix_w), BF16),
        compiler_params=_cparams("parallel"),
        name="swa_attn",
    )(sink, q, k, v)


def _merge_kernel(x_ref, g_ref, wg_ref, bg_ref, ya_ref, yb_ref, yc_ref, yd_ref, wb_ref, wo_ref, o_ref):
    x = x_ref[...]
    d_model = x.shape[1]
    hb = _rmsnorm_rows(x, g_ref[...]).astype(BF16)
    mixed = jnp.zeros(x.shape, F32)
    for n, y_ref in enumerate((ya_ref, yb_ref, yc_ref, yd_ref)):
        cols = slice(n * d_model, (n + 1) * d_model)
        gate = jax.nn.sigmoid(jnp.dot(hb, wg_ref[:, cols], preferred_element_type=F32) + bg_ref[:, cols])
        mixed = mixed + gate * jnp.dot(y_ref[...], wb_ref[n], preferred_element_type=F32)
    o_ref[...] = x + jnp.dot(mixed.astype(BF16), wo_ref[...], preferred_element_type=F32)


def _merge(x2, g, w_gate, b_gate, ys, w_branch, w_out, *, tm):
    tokens, d_model = x2.shape
    mix_w = ys[0].shape[1]
    tok_spec = lambda w: pl.BlockSpec((tm, w), lambda i: (i, 0))
    full = lambda shape: pl.BlockSpec(shape, lambda i: (0,) * len(shape))
    return pl.pallas_call(
        _merge_kernel,
        grid=(tokens // tm,),
        in_specs=[tok_spec(d_model), full(g.shape), full(w_gate.shape), full(b_gate.shape)]
                 + [tok_spec(mix_w)] * N_BRANCH
                 + [full(w_branch.shape), full(w_out.shape)],
        out_specs=tok_spec(d_model),
        out_shape=jax.ShapeDtypeStruct((tokens, d_model), F32),
        compiler_params=_cparams("parallel"),
        name="merge",
    )(x2, g, w_gate, b_gate, *ys, w_branch, w_out)


FFN_CHUNK = 256


FFN_SLOTS = 2
FFN_TM = 512
FFN_DOWN_GROUP = 4


def _ffn_kernel(x_ref, xp_ref, xn_ref, g_ref, wu_ref, cw_ref, cb_ref, wd_ref, o_ref, u_ref, act_ref, *, d_ff):
    j = pl.program_id(1)
    tm = x_ref.shape[0]
    g = g_ref[...]
    hp = jnp.where(j > 0, _rmsnorm_rows(xp_ref[...], g), 0.0)
    hn = jnp.where(j < pl.num_programs(1) - 1, _rmsnorm_rows(xn_ref[...], g), 0.0)
    hcat = jnp.concatenate([hp, _rmsnorm_rows(x_ref[...], g), hn], axis=0).astype(BF16)

    n_chunk = d_ff // FFN_CHUNK
    n_slab = FFN_CHUNK // LANES

    def up_project(c):
        u = jnp.dot(hcat, wu_ref[:, c * 2 * FFN_CHUNK:(c + 1) * 2 * FFN_CHUNK], preferred_element_type=F32)
        for s in range(2 * n_slab):
            u_ref[c % FFN_SLOTS, s] = u[:, s * LANES:(s + 1) * LANES]

    def conv_act(c):
        def conv(s):
            lanes = slice(c * 2 * FFN_CHUNK + s * LANES, c * 2 * FFN_CHUNK + (s + 1) * LANES)
            y = cb_ref[:, lanes]
            for k in range(FFN_CONV_K):
                rows = slice(SUBLANES - 1 + k, SUBLANES - 1 + k + tm)
                y = y + u_ref[c % FFN_SLOTS, s, rows, :] * cw_ref[k:k + 1, lanes]
            return y

        for s in range(n_slab):
            gate, up = conv(s), conv(n_slab + s)
            out_lanes = slice(c * FFN_CHUNK + s * LANES, c * FFN_CHUNK + (s + 1) * LANES)
            act_ref[:, out_lanes] = (gate * jax.nn.sigmoid(gate) * up).astype(BF16)

    out = x_ref[...]
    group_start = 0
    up_project(0)
    for c in range(n_chunk):
        if c + 1 < n_chunk:
            up_project(c + 1)
        conv_act(c)
        if (c + 1) % FFN_DOWN_GROUP == 0 or c + 1 == n_chunk:
            rows = slice(group_start * FFN_CHUNK, (c + 1) * FFN_CHUNK)
            out = out + jnp.dot(act_ref[:, rows], wd_ref[rows, :], preferred_element_type=F32)
            group_start = c + 1
    o_ref[...] = out


def _interleave_gate_up(a, d_ff):
    lead = a.shape[:-1]
    a = a.reshape(*lead, 2, d_ff // FFN_CHUNK, FFN_CHUNK)
    return jnp.swapaxes(a, -3, -2).reshape(*lead, 2 * d_ff)


def _ffn(x2, g, w_up, conv_w, conv_b, w_down, *, seq, tm):
    tokens, d_model = x2.shape
    d_ff = w_down.shape[0]
    seq_tiles = seq // tm
    halo_per_tile = tm // SUBLANES
    n_halo = tokens // SUBLANES
    full = lambda shape: pl.BlockSpec(shape, lambda b, j: (0,) * len(shape), pipeline_mode=pl.Buffered(1))
    return pl.pallas_call(
        functools.partial(_ffn_kernel, d_ff=d_ff),
        grid=(tokens // seq, seq_tiles),
        in_specs=[
            pl.BlockSpec((tm, d_model), lambda b, j: (b * seq_tiles + j, 0)),
            pl.BlockSpec((SUBLANES, d_model),
                         lambda b, j: (jnp.maximum((b * seq_tiles + j) * halo_per_tile - 1, 0), 0)),
            pl.BlockSpec((SUBLANES, d_model),
                         lambda b, j: (jnp.minimum((b * seq_tiles + j + 1) * halo_per_tile, n_halo - 1), 0)),
            full(g.shape), full(w_up.shape), full(conv_w.shape), full(conv_b.shape), full(w_down.shape),
        ],
        out_specs=pl.BlockSpec((tm, d_model), lambda b, j: (b * seq_tiles + j, 0)),
        out_shape=jax.ShapeDtypeStruct((tokens, d_model), F32),
        scratch_shapes=[pltpu.VMEM((FFN_SLOTS, 2 * FFN_CHUNK // LANES, tm + 2 * SUBLANES, LANES), F32),
                        pltpu.VMEM((tm, d_ff), BF16)],
        compiler_params=_cparams("parallel", "parallel"),
        name="conv_ffn",
    )(x2, x2, x2, g, w_up, conv_w, conv_b, w_down)


def _rope_lane_tables(seq):
    half = ROPE_DIMS // 2
    pos = jnp.arange(seq, dtype=F32)
    inv = ROPE_THETA ** (-jnp.arange(0, ROPE_DIMS, 2, dtype=F32) / ROPE_DIMS)
    ang = pos[:, None] * inv[None, :]
    cos, sin = jnp.cos(ang), jnp.sin(ang)
    d = np.arange(LANES) % HEAD_DIM
    first, second = d < half, (d >= half) & (d < ROPE_DIMS)
    idx = d % half
    c_tab = jnp.where((first | second)[None, :], cos[:, idx], 1.0)
    a_tab = jnp.where(first[None, :], -sin[:, idx], 0.0)
    b_tab = jnp.where(second[None, :], sin[:, idx], 0.0)
    return jnp.stack([c_tab, a_tab, b_tab]).astype(F32)


def kernel(x, g_mix, w_in, gate_b, a_conv_w, a_conv_b, a_ln_g, a_ln_b, na_qn, na_kn, na_rpb, dil_qn, dil_kn, swa_qn, swa_kn, swa_sink, w_branch, w_out, g_ffn, w_up, ffn_conv_w, ffn_conv_b, w_down):
    bsz, seq, d_model = x.shape
    depth = w_in.shape[0]
    mix_w = d_model // N_BRANCH
    heads = mix_w // HEAD_DIM
    kv_heads = swa_sink.shape[1] // 2
    grp = heads // kv_heads
    n_gate = N_BRANCH * d_model
    assert mix_w % LANES == 0 and seq % (GRID_W * NA_ROWS) == 0
    assert all(w // (2 * d) == DIL_PAIRS[0][0] // 2 for w, d in DIL_PAIRS)

    rope_tab = _rope_lane_tables(seq)
    scale = HEAD_DIM ** -0.5
    tile_h = lambda v: jnp.tile(v.astype(F32), heads)
    x2 = x.reshape(bsz * seq, d_model)
    na_bias = _na_bias_table(na_rpb, seq // GRID_W)

    for l in range(depth):
        w_l = w_in[l]
        o_swk = w_l.shape[1] - 2 * kv_heads * HEAD_DIM
        rep_heads = lambda w: jnp.repeat(w.reshape(d_model, kv_heads, HEAD_DIM), grp, axis=1).reshape(d_model, mix_w)
        w_br = jnp.concatenate([
            w_l[:, n_gate:o_swk],
            rep_heads(w_l[:, o_swk:o_swk + kv_heads * HEAD_DIM]),
            rep_heads(w_l[:, o_swk + kv_heads * HEAD_DIM:]),
        ], axis=1).astype(BF16)
        gains = jnp.stack([tile_h(na_qn[l]) * scale, tile_h(na_kn[l]),
                           tile_h(dil_qn[l]) * scale, tile_h(dil_kn[l]),
                           tile_h(swa_qn[l]) * scale, tile_h(swa_kn[l]),
                           jnp.zeros((mix_w,), F32), jnp.zeros((mix_w,), F32)])

        a_u, na_q, na_k, na_v, dil_qkv, sw_q, sw_k, sw_v = _inproj(
            x2, g_mix[l][None, :], w_br, gains, rope_tab, seq=seq, tm=512)
        y_a = _confconv(a_u, a_conv_w[l], a_conv_b[l][None, :], a_ln_g[l][None, :], a_ln_b[l][None, :], seq=seq)
        y_b = _na_attn(na_q, na_k, na_v, na_bias[l], seq=seq)
        y_c = _dilated_attn(dil_qkv, seq=seq, mix_w=mix_w)
        y_d = _swa_attn(swa_sink[l].astype(F32), sw_q, sw_k, sw_v, seq=seq)
        x2 = _merge(x2, g_mix[l][None, :], w_l[:, :n_gate].astype(BF16), gate_b[l].reshape(1, n_gate),
                    (y_a, y_b, y_c, y_d), w_branch[l].astype(BF16), w_out[l].astype(BF16), tm=512)
        d_ff = w_down.shape[1]
        x2 = _ffn(x2, g_ffn[l][None, :], _interleave_gate_up(w_up[l], d_ff).astype(BF16),
                  _interleave_gate_up(ffn_conv_w[l], d_ff), _interleave_gate_up(ffn_conv_b[l][None, :], d_ff),
                  w_down[l].astype(BF16), seq=seq, tm=FFN_TM)
    return x2.reshape(bsz, seq, d_model)
```

```python
import functools

import numpy as np
import jax
import jax.numpy as jnp
from jax import lax
from jax.experimental import pallas as pl
from jax.experimental.pallas import tpu as pltpu

F32 = jnp.float32
BF16 = jnp.bfloat16

N_BRANCH = 4
HEAD_DIM = 64
CONV_A_K = 31
NA_ROWS = 8
NA_COLS = 16
GRID_W = 64
DIL_PAIRS = ((128, 1), (512, 4), (2048, 16))
SWA_WINDOW = 128
Q_BLOCK = 128
ROPE_THETA = 500000.0
ROPE_DIMS = HEAD_DIM // 4
FFN_CONV_K = 3
EPS = 1e-6
NEG_INF = -1e30
LOG2_E = 1.4426950408889634

LANES = 128
SUBLANES = 8
VMEM_LIMIT = 56 * 1024 * 1024

HEADS_PER_SLAB = LANES // HEAD_DIM


def _cparams(*sem):
    return pltpu.CompilerParams(dimension_semantics=sem, vmem_limit_bytes=VMEM_LIMIT)


def _low_head_mask():
    return lax.broadcasted_iota(jnp.int32, (1, LANES), 1) < HEAD_DIM


def _rmsnorm_rows(x, g):
    return x * lax.rsqrt(jnp.mean(x * x, axis=-1, keepdims=True) + EPS) * g


def _rope(y, rope_ref):
    half = ROPE_DIMS // 2
    return (y * rope_ref[0]
            + pltpu.roll(y, LANES - half, 1) * rope_ref[1]
            + pltpu.roll(y, half, 1) * rope_ref[2])


(SEG_GLU_A, SEG_GLU_G, SEG_NA_Q, SEG_NA_K, SEG_NA_V, SEG_DIL_Q, SEG_DIL_K, SEG_DIL_V,
 SEG_SW_Q, SEG_SW_K, SEG_SW_V) = range(11)
NORMED_SEGS = (SEG_NA_Q, SEG_NA_K, SEG_DIL_Q, SEG_DIL_K, SEG_SW_Q, SEG_SW_K)


def _inproj_kernel(x_ref, g_ref, w_ref, gains_ref, rope_ref,
                   a_ref, naq_ref, nak_ref, nav_ref, dil_ref, swq_ref, swk_ref, swv_ref, *, mix_w):
    n_slab = mix_w // LANES
    hb = _rmsnorm_rows(x_ref[...], g_ref[...]).astype(BF16)
    r = lax.broadcasted_iota(jnp.int32, (mix_w, mix_w), 0) // HEAD_DIM
    c = lax.broadcasted_iota(jnp.int32, (mix_w, mix_w), 1) // HEAD_DIM
    blockdiag = (r == c).astype(BF16)

    def proj(seg):
        return jnp.dot(hb, w_ref[:, seg * mix_w:(seg + 1) * mix_w], preferred_element_type=F32)

    def head_norm(row, v):
        ms = jnp.dot((v * v).astype(BF16), blockdiag, preferred_element_type=F32) * (1.0 / HEAD_DIM)
        return v * lax.rsqrt(ms + EPS) * gains_ref[row:row + 1, :]

    p, normed = {}, {}
    for row, seg in enumerate(NORMED_SEGS):
        p[seg] = proj(seg)
        if row:
            normed[NORMED_SEGS[row - 1]] = head_norm(row - 1, p[NORMED_SEGS[row - 1]])
    for seg in range(SEG_SW_V + 1):
        if seg not in p:
            p[seg] = proj(seg)
            if NORMED_SEGS[-1] not in normed:
                normed[NORMED_SEGS[-1]] = head_norm(len(NORMED_SEGS) - 1, p[NORMED_SEGS[-1]])

    def slabs(v):
        return [v[:, s * LANES:(s + 1) * LANES] for s in range(n_slab)]

    a_ref[...] = p[SEG_GLU_A] * jax.nn.sigmoid(p[SEG_GLU_G])
    naq_ref[...] = normed[SEG_NA_Q].astype(BF16)
    nak_ref[...] = normed[SEG_NA_K].astype(BF16)
    nav_ref[...] = p[SEG_NA_V].astype(BF16)
    for s in range(n_slab):
        dil_ref[s] = _rope(slabs(normed[SEG_DIL_Q])[s], rope_ref)
        dil_ref[n_slab + s] = _rope(slabs(normed[SEG_DIL_K])[s], rope_ref)
        dil_ref[2 * n_slab + s] = slabs(p[SEG_DIL_V])[s]
        lanes = slice(s * LANES, (s + 1) * LANES)
        swq_ref[:, lanes] = _rope(slabs(normed[SEG_SW_Q])[s], rope_ref).astype(BF16)
        swk_ref[:, lanes] = _rope(slabs(normed[SEG_SW_K])[s], rope_ref).astype(BF16)
    swv_ref[...] = p[SEG_SW_V].astype(BF16)


def _inproj(x2, g, w_br, gains, rope_tab, *, seq, tm):
    tokens, d_model = x2.shape
    n_br = w_br.shape[1]
    mix_w = gains.shape[1]
    n_slab = mix_w // LANES
    seq_tiles = seq // tm
    tok_spec = lambda w: pl.BlockSpec((tm, w), lambda i: (i, 0))
    full = lambda shape: pl.BlockSpec(shape, lambda i: (0,) * len(shape))
    out_shape = (
        jax.ShapeDtypeStruct((tokens, mix_w), F32),
        jax.ShapeDtypeStruct((tokens, mix_w), BF16),
        jax.ShapeDtypeStruct((tokens, mix_w), BF16),
        jax.ShapeDtypeStruct((tokens, mix_w), BF16),
        jax.ShapeDtypeStruct((3 * n_slab, tokens, LANES), F32),
        jax.ShapeDtypeStruct((tokens, mix_w), BF16),
        jax.ShapeDtypeStruct((tokens, mix_w), BF16),
        jax.ShapeDtypeStruct((tokens, mix_w), BF16),
    )
    out_specs = (
        tok_spec(mix_w), tok_spec(mix_w), tok_spec(mix_w), tok_spec(mix_w),
        pl.BlockSpec((3 * n_slab, tm, LANES), lambda i: (0, i, 0)),
        tok_spec(mix_w), tok_spec(mix_w), tok_spec(mix_w),
    )
    return pl.pallas_call(
        functools.partial(_inproj_kernel, mix_w=mix_w),
        grid=(tokens // tm,),
        in_specs=[
            tok_spec(d_model),
            full((1, d_model)),
            full((d_model, n_br)),
            full(gains.shape),
            pl.BlockSpec((3, tm, LANES), lambda i: (0, i % seq_tiles, 0)),
        ],
        out_specs=out_specs,
        out_shape=out_shape,
        compiler_params=_cparams("parallel"),
        name="inproj",
    )(x2, g, w_br, gains, rope_tab)


CONV_CHUNK = 128
CONV_PAD = 16


NA_GROUP = 4
DIL_GROUP = 4
SWA_GROUP = 2


def _stack_heads(q_slab, low):
    zero = jnp.zeros_like(q_slab)
    return jnp.concatenate([jnp.where(low, q_slab, zero), jnp.where(low, zero, q_slab)], axis=0)


def _unstack_heads(o, low):
    m = o.shape[0] // HEADS_PER_SLAB
    return jnp.where(low, o[:m], o[m:])


def _qk(q_stack, k_slab):
    return lax.dot_general(q_stack, k_slab, (((1,), (1,)), ((), ())), preferred_element_type=F32)


def _band_bias(nq, nk, shift, side):
    qi = lax.broadcasted_iota(jnp.int32, (HEADS_PER_SLAB * nq, nk), 0) % nq
    ki = lax.broadcasted_iota(jnp.int32, (HEADS_PER_SLAB * nq, nk), 1)
    return jnp.where(jnp.abs(qi + shift - ki) <= side, 0.0, NEG_INF).astype(F32)


def _fill_band_biases(mask_ref, nq, side):
    for c in range(mask_ref.shape[0]):
        mask_ref[c] = _band_bias(nq, mask_ref.shape[2], c * side, side)


def _na_kernel(q_ref, k_ref, v_ref, bias_ref, o_ref, *, rows, n_slab):
    low = _low_head_mask()
    kr = NA_ROWS
    n_edge_lo = kr // 2 + 1
    mid_hi = rows - (kr - kr // 2)

    def row_group(g, carry):
        chains = []
        for i in range(NA_GROUP):
            r = g * NA_GROUP + i
            row_start = jnp.clip(r - kr // 2, 0, rows - kr)
            cls = jnp.where(r < n_edge_lo, r, jnp.where(r <= mid_hi, n_edge_lo, r - mid_hi + n_edge_lo))
            q0 = pl.multiple_of(r * GRID_W, GRID_W)
            k0 = pl.multiple_of(row_start * GRID_W, GRID_W)
            for s in range(n_slab):
                chains.append((cls, s, pl.ds(q0, GRID_W), pl.ds(k0, kr * GRID_W),
                               slice(s * LANES, (s + 1) * LANES)))
        scores = [_qk(_stack_heads(q_ref[qrows, lanes], low), k_ref[krows, lanes]) + bias_ref[cls, s]
                  for cls, s, qrows, krows, lanes in chains]
        probs, dens = [], []
        for sc in scores:
            p = jnp.exp2(sc - jnp.max(sc, axis=-1, keepdims=True))
            dens.append(jnp.sum(p, axis=-1, keepdims=True))
            probs.append(p.astype(BF16))
        for (cls, s, qrows, krows, lanes), p, l in zip(chains, probs, dens):
            pv = jnp.dot(p, v_ref[krows, lanes], preferred_element_type=F32)
            o_ref[qrows, lanes] = _unstack_heads(pv / l, low).astype(o_ref.dtype)
        return carry

    lax.fori_loop(0, rows // NA_GROUP, row_group, 0)


def _na_bias_table(rpb, rows):
    kr, kc = NA_ROWS, NA_COLS
    depth, heads = rpb.shape[:2]
    n_edge_lo = kr // 2 + 1
    mid_hi = rows - (kr - kr // 2)
    rep = np.asarray(list(range(n_edge_lo)) + [n_edge_lo] + list(range(mid_hi + 1, rows)))
    row_start = np.clip(rep - kr // 2, 0, rows - kr)
    dr = row_start[:, None] + np.arange(kr)[None, :] - rep[:, None] + (NA_ROWS - 1)
    col = np.arange(GRID_W)
    col_start = np.clip(col - kc // 2, 0, GRID_W - kc)
    valid = (col[None, :] >= col_start[:, None]) & (col[None, :] < col_start[:, None] + kc)
    dc = col[None, :] - col[:, None] + (NA_COLS - 1)
    sel_r = (dr[:, :, None] == np.arange(2 * NA_ROWS - 1)).astype(np.float32)
    sel_c = ((dc[:, :, None] == np.arange(2 * NA_COLS - 1)) & valid[:, :, None]).astype(np.float32)
    tab = jnp.einsum('kir,lhrj,wcj->lkhwic', sel_r, rpb.astype(F32), sel_c,
                     precision=lax.Precision.HIGHEST)
    tab = tab + jnp.where(valid, 0.0, NEG_INF).astype(F32)[None, None, None, :, None, :]
    return tab.reshape(depth, len(rep), heads // HEADS_PER_SLAB, HEADS_PER_SLAB * GRID_W, kr * GRID_W)


def _na_attn(q, k, v, bias, *, seq):
    tokens, mix_w = q.shape
    seq_spec = pl.BlockSpec((seq, mix_w), lambda i: (i, 0))
    return pl.pallas_call(
        functools.partial(_na_kernel, rows=seq // GRID_W, n_slab=mix_w // LANES),
        grid=(tokens // seq,),
        in_specs=[seq_spec, seq_spec, seq_spec,
                  pl.BlockSpec(bias.shape, lambda i: (0, 0, 0, 0))],
        out_specs=seq_spec,
        out_shape=jax.ShapeDtypeStruct((tokens, mix_w), BF16),
        compiler_params=_cparams("parallel"),
        name="na_attn",
    )(q, k, v, bias)


def _dil_group_blocks(chains, low):
    scores = [_qk(_stack_heads(lq().astype(BF16), low), lk().astype(BF16)) + lb()
              for lq, lk, _, lb in chains]
    stats = []
    for sc in scores:
        m = jnp.max(sc, axis=-1, keepdims=True)
        p = jnp.exp2(sc - m)
        stats.append((p.astype(BF16), m, jnp.sum(p, axis=-1, keepdims=True)))
    outs = []
    for (_, _, lv, _), (p, m, l) in zip(chains, stats):
        acc = jnp.dot(p, lv().astype(BF16), preferred_element_type=F32)
        nq = acc.shape[0] // HEADS_PER_SLAB
        outs.append((_unstack_heads(acc, low), jnp.where(low, m[:nq], m[nq:]), jnp.where(low, l[:nq], l[nq:])))
    return outs


def _dilated_kernel(qkv_ref, o_ref, acc_ref, m_ref, l_ref, mask_ref, mask_sq_ref, *, n_slab):
    seq = qkv_ref.shape[1]
    low = _low_head_mask()
    (_, dil1), (_, dil2), (_, dil3) = DIL_PAIRS
    side = DIL_PAIRS[0][0] // (2 * dil1)
    qb = Q_BLOCK
    len2, len3 = seq // dil2, seq // dil3
    nblk2 = len2 // qb
    _fill_band_biases(mask_ref, qb, side)
    _fill_band_biases(mask_sq_ref, len3, side)

    def chain(s, qrows, krows, bias):
        load = lambda kind, rows: (lambda: qkv_ref[kind * n_slab + s, rows, :])
        return (load(0, qrows), load(1, krows), load(2, krows), bias)

    def wide_bias(a0, k0):
        return lambda: mask_ref[(a0 - k0) // side]

    def g3(g, carry):
        blocks = [(s, pl.ds(g * DIL_GROUP + i, len3, stride=dil3))
                  for i in range(DIL_GROUP) for s in range(n_slab)]
        outs = _dil_group_blocks([chain(s, rows, rows, lambda: mask_sq_ref[0]) for s, rows in blocks], low)
        for (s, rows), (acc, m, l) in zip(blocks, outs):
            acc_ref[n_slab + s, rows, :], m_ref[n_slab + s, rows, :], l_ref[n_slab + s, rows, :] = acc, m, l
        return carry

    lax.fori_loop(0, dil3 // DIL_GROUP, g3, 0)

    def g2(g, carry):
        blocks, chains = [], []
        for i in range(DIL_GROUP):
            t = g * DIL_GROUP + i
            b = t // nblk2
            a0 = (t % nblk2) * qb
            k0 = jnp.clip(a0 - side, 0, len2 - 2 * qb)
            qrows = pl.ds(b + dil2 * a0, qb, stride=dil2)
            krows = pl.ds(b + dil2 * k0, 2 * qb, stride=dil2)
            for s in range(n_slab):
                blocks.append((s, qrows))
                chains.append(chain(s, qrows, krows, wide_bias(a0, k0)))
        for (s, qrows), (acc, m, l) in zip(blocks, _dil_group_blocks(chains, low)):
            acc_ref[s, qrows, :], m_ref[s, qrows, :], l_ref[s, qrows, :] = acc, m, l
        return carry

    lax.fori_loop(0, dil2 * nblk2 // DIL_GROUP, g2, 0)

    def g1(g, carry):
        blocks, chains = [], []
        for i in range(DIL_GROUP):
            a0 = pl.multiple_of((g * DIL_GROUP + i) * qb, qb)
            k0 = pl.multiple_of(jnp.clip(a0 - side, 0, seq - 2 * qb), side)
            for s in range(n_slab):
                blocks.append((s, pl.ds(a0, qb)))
                chains.append(chain(s, pl.ds(a0, qb), pl.ds(k0, 2 * qb), wide_bias(a0, k0)))
        for (s, qrows), (acc1, m1, l1) in zip(blocks, _dil_group_blocks(chains, low)):
            m2, m3 = m_ref[s, qrows, :], m_ref[n_slab + s, qrows, :]
            m = jnp.maximum(jnp.maximum(m1, m2), m3)
            w1, w2, w3 = jnp.exp2(m1 - m), jnp.exp2(m2 - m), jnp.exp2(m3 - m)
            num = w1 * acc1 + w2 * acc_ref[s, qrows, :] + w3 * acc_ref[n_slab + s, qrows, :]
            den = w1 * l1 + w2 * l_ref[s, qrows, :] + w3 * l_ref[n_slab + s, qrows, :]
            o_ref[qrows, s * LANES:(s + 1) * LANES] = (num / den).astype(o_ref.dtype)
        return carry

    lax.fori_loop(0, seq // qb // DIL_GROUP, g1, 0)


def _dilated_attn(qkv, *, seq, mix_w):
    n3, tokens, _ = qkv.shape
    n_slab = n3 // 3
    return pl.pallas_call(
        functools.partial(_dilated_kernel, n_slab=n_slab),
        grid=(tokens // seq,),
        in_specs=[pl.BlockSpec((n3, seq, LANES), lambda i: (0, i, 0))],
        out_specs=pl.BlockSpec((seq, mix_w), lambda i: (i, 0)),
        out_shape=jax.ShapeDtypeStruct((tokens, mix_w), BF16),
        scratch_shapes=[pltpu.VMEM((2 * n_slab, seq, LANES), F32)] * 3 + [
            pltpu.VMEM((3, HEADS_PER_SLAB * Q_BLOCK, 2 * Q_BLOCK), F32),
            pltpu.VMEM((1, HEADS_PER_SLAB * (seq // DIL_PAIRS[2][1]), seq // DIL_PAIRS[2][1]), F32)],
        compiler_params=_cparams("parallel"),
        name="dilated_attn",
    )(qkv)


def _swa_kernel(sink_ref, q_ref, k_ref, v_ref, o_ref, mask_ref, *, n_slab):
    seq = q_ref.shape[0]
    low = _low_head_mask()
    qb = Q_BLOCK
    nk = qb + 2 * SWA_WINDOW
    _fill_band_biases(mask_ref, qb, SWA_WINDOW)
    first_head = lax.broadcasted_iota(jnp.int32, (HEADS_PER_SLAB * qb, 1), 0) < qb

    def block_group(g, carry):
        chains = []
        for i in range(SWA_GROUP):
            a0 = pl.multiple_of((g * SWA_GROUP + i) * qb, qb)
            k0 = pl.multiple_of(jnp.clip(a0 - SWA_WINDOW, 0, seq - nk), qb)
            cls = (a0 - k0) // SWA_WINDOW
            for s in range(n_slab):
                chains.append((s, pl.ds(a0, qb), pl.ds(k0, nk), slice(s * LANES, (s + 1) * LANES), cls))
        scores = [_qk(_stack_heads(q_ref[qrows, lanes], low), k_ref[krows, lanes]) + mask_ref[cls]
                  for s, qrows, krows, lanes, cls in chains]
        probs, dens = [], []
        for (s, _, _, _, _), sc in zip(chains, scores):
            sink = jnp.where(first_head, sink_ref[s * HEADS_PER_SLAB], sink_ref[s * HEADS_PER_SLAB + 1])
            m = jnp.maximum(jnp.max(sc, axis=-1, keepdims=True), sink)
            p = jnp.exp2(sc - m)
            dens.append(jnp.sum(p, axis=-1, keepdims=True) + jnp.exp2(sink - m))
            probs.append(p.astype(BF16))
        for (s, qrows, krows, lanes, _), p, denom in zip(chains, probs, dens):
            pv = jnp.dot(p, v_ref[krows, lanes], preferred_element_type=F32)
            o_ref[qrows, lanes] = _unstack_heads(pv / denom, low).astype(o_ref.dtype)
        return carry

    lax.fori_loop(0, seq // qb // SWA_GROUP, block_group, 0)


def _swa_attn(sink, q, k, v, *, seq):
    tokens, mix_w = q.shape
    seq_spec = pl.BlockSpec((seq, mix_w), lambda i: (i, 0))
    return pl.pallas_call(
        functools.partial(_swa_kernel, n_slab=mix_w // LANES),
        grid=(tokens // seq,),
        in_specs=[pl.BlockSpec(memory_space=pltpu.SMEM), seq_spec, seq_spec, seq_spec],
        out_specs=seq_spec,
        out_shape=jax.ShapeDtypeStruct((tokens, mix_w), BF16),
        scratch_shapes=[pltpu.VMEM((3, HEADS_PER_SLAB * Q_BLOCK, Q_BLOCK + 2 * SWA_WINDOW), F32)],
        compiler_params=_cparams("parallel"),
        name="swa_attn",
    )(sink, q, k, v)


def _conformer_rows(pad_ref, ya_ref, cw_ref, cb_ref, lg_ref, lb_ref, row_blocks):
    n_slab = pad_ref.shape[0]
    ch = n_slab * LANES
    half = (CONV_A_K - 1) // 2
    for c0 in row_blocks:
        accs = []
        for s in range(n_slab):
            lanes = slice(s * LANES, (s + 1) * LANES)
            acc = cb_ref[:, lanes]
            for k in range(CONV_A_K):
                r0 = c0 + CONV_PAD - half + k
                acc = acc + pad_ref[s, r0:r0 + CONV_CHUNK, :] * cw_ref[k:k + 1, lanes]
            accs.append(acc)
        mu = sum(jnp.sum(a, axis=-1, keepdims=True) for a in accs) * (1.0 / ch)
        ds = [a - mu for a in accs]
        var = sum(jnp.sum(d * d, axis=-1, keepdims=True) for d in ds) * (1.0 / ch)
        inv = lax.rsqrt(var + EPS)
        for s, d in enumerate(ds):
            lanes = slice(s * LANES, (s + 1) * LANES)
            y = d * inv * lg_ref[:, lanes] + lb_ref[:, lanes]
            ya_ref[c0:c0 + CONV_CHUNK, lanes] = (y * jax.nn.sigmoid(y)).astype(ya_ref.dtype)


def _merge_kernel(x_ref, g_ref, wg_ref, bg_ref, au_ref, aup_ref, aun_ref, cw_ref, cb_ref, lg_ref, lb_ref,
                  yb_ref, yc_ref, yd_ref, wb_ref, wo_ref, o_ref, pad_ref, ya_ref, *, seq_tiles):
    tm, d_model = x_ref.shape
    j = pl.program_id(0) % seq_tiles
    for s in range(pad_ref.shape[0]):
        lanes = slice(s * LANES, (s + 1) * LANES)
        pad_ref[s, 0:CONV_PAD, :] = jnp.where(j > 0, aup_ref[:, lanes], 0.0)
        pad_ref[s, CONV_PAD:CONV_PAD + tm, :] = au_ref[:, lanes]
        pad_ref[s, CONV_PAD + tm:2 * CONV_PAD + tm, :] = jnp.where(j < seq_tiles - 1, aun_ref[:, lanes], 0.0)

    x = x_ref[...]
    hb = _rmsnorm_rows(x, g_ref[...]).astype(BF16)

    def gate(n):
        cols = slice(n * d_model, (n + 1) * d_model)
        return jax.nn.sigmoid(jnp.dot(hb, wg_ref[:, cols], preferred_element_type=F32) + bg_ref[:, cols])

    def branch(n, y):
        return jnp.dot(y, wb_ref[n], preferred_element_type=F32)

    blocks = list(range(0, tm, CONV_CHUNK))
    others = [(1, yb_ref), (2, yc_ref), (3, yd_ref)]
    gates = {0: gate(0)}
    big = ([lambda n=n: gates.__setitem__(n, gate(n)) for n, _ in others]
           + [lambda n=n, y_ref=y_ref: gates[n] * branch(n, y_ref[...]) for n, y_ref in others])
    a_terms, other_terms = [], []
    for i in range(max(len(blocks), len(big))):
        if i < len(big):
            out = big[i]()
            if out is not None:
                other_terms.append(out)
        if i < len(blocks):
            c0 = blocks[i]
            _conformer_rows(pad_ref, ya_ref, cw_ref, cb_ref, lg_ref, lb_ref, [c0])
            a_terms.append(gates[0][c0:c0 + CONV_CHUNK] * branch(0, ya_ref[c0:c0 + CONV_CHUNK, :]))
    mixed = jnp.concatenate(a_terms, axis=0)
    for t in other_terms:
        mixed = mixed + t
    o_ref[...] = x + jnp.dot(mixed.astype(BF16), wo_ref[...], preferred_element_type=F32)


def _merge(x2, g, w_gate, b_gate, a_u, conv_w, conv_b, ln_g, ln_b, ys, w_branch, w_out, *, seq, tm):
    tokens, d_model = x2.shape
    mix_w = a_u.shape[1]
    seq_tiles = seq // tm
    halo_per_tile = tm // CONV_PAD
    n_halo = tokens // CONV_PAD
    tok_spec = lambda w: pl.BlockSpec((tm, w), lambda i: (i, 0))
    full = lambda shape: pl.BlockSpec(shape, lambda i: (0,) * len(shape))
    return pl.pallas_call(
        functools.partial(_merge_kernel, seq_tiles=seq_tiles),
        grid=(tokens // tm,),
        in_specs=[tok_spec(d_model), full(g.shape), full(w_gate.shape), full(b_gate.shape),
                  tok_spec(mix_w),
                  pl.BlockSpec((CONV_PAD, mix_w), lambda i: (jnp.maximum(i * halo_per_tile - 1, 0), 0)),
                  pl.BlockSpec((CONV_PAD, mix_w), lambda i: (jnp.minimum((i + 1) * halo_per_tile, n_halo - 1), 0)),
                  full(conv_w.shape), full(conv_b.shape), full(ln_g.shape), full(ln_b.shape)]
                 + [tok_spec(mix_w)] * len(ys)
                 + [full(w_branch.shape), full(w_out.shape)],
        out_specs=tok_spec(d_model),
        out_shape=jax.ShapeDtypeStruct((tokens, d_model), F32),
        scratch_shapes=[pltpu.VMEM((mix_w // LANES, tm + 2 * CONV_PAD, LANES), F32),
                        pltpu.VMEM((tm, mix_w), BF16)],
        compiler_params=_cparams("parallel"),
        name="merge",
    )(x2, g, w_gate, b_gate, a_u, a_u, a_u, conv_w, conv_b, ln_g, ln_b, *ys, w_branch, w_out)


FFN_CHUNK = 256


FFN_SLOTS = 2
FFN_TM = 512
FFN_DOWN_GROUP = 4


def _ffn_kernel(x_ref, xp_ref, xn_ref, g_ref, wu_ref, cw_ref, cb_ref, wd_ref, o_ref, u_ref, act_ref, *, d_ff):
    j = pl.program_id(1)
    tm = x_ref.shape[0]
    g = g_ref[...]
    hp = jnp.where(j > 0, _rmsnorm_rows(xp_ref[...], g), 0.0)
    hn = jnp.where(j < pl.num_programs(1) - 1, _rmsnorm_rows(xn_ref[...], g), 0.0)
    hcat = jnp.concatenate([hp, _rmsnorm_rows(x_ref[...], g), hn], axis=0).astype(BF16)

    n_chunk = d_ff // FFN_CHUNK
    n_slab = FFN_CHUNK // LANES

    def col0(c, half):
        return half * d_ff + c * FFN_CHUNK

    def up_project(c):
        for half in range(2):
            u = jnp.dot(hcat, wu_ref[:, col0(c, half):col0(c, half) + FFN_CHUNK], preferred_element_type=F32)
            for s in range(n_slab):
                u_ref[c % FFN_SLOTS, half * n_slab + s] = u[:, s * LANES:(s + 1) * LANES]

    def conv_act(c):
        def conv(half, s):
            lanes = slice(col0(c, half) + s * LANES, col0(c, half) + (s + 1) * LANES)
            y = cb_ref[:, lanes]
            for k in range(FFN_CONV_K):
                rows = slice(SUBLANES - 1 + k, SUBLANES - 1 + k + tm)
                y = y + u_ref[c % FFN_SLOTS, half * n_slab + s, rows, :] * cw_ref[k:k + 1, lanes]
            return y

        for s in range(n_slab):
            gate, up = conv(0, s), conv(1, s)
            out_lanes = slice(c * FFN_CHUNK + s * LANES, c * FFN_CHUNK + (s + 1) * LANES)
            act_ref[:, out_lanes] = (gate * jax.nn.sigmoid(gate) * up).astype(BF16)

    out = x_ref[...]
    group_start = 0
    up_project(0)
    for c in range(n_chunk):
        if c + 1 < n_chunk:
            up_project(c + 1)
        conv_act(c)
        if (c + 1) % FFN_DOWN_GROUP == 0 or c + 1 == n_chunk:
            rows = slice(group_start * FFN_CHUNK, (c + 1) * FFN_CHUNK)
            out = out + jnp.dot(act_ref[:, rows], wd_ref[rows, :], preferred_element_type=F32)
            group_start = c + 1
    o_ref[...] = out


def _ffn(x2, g, w_up, conv_w, conv_b, w_down, *, seq, tm):
    tokens, d_model = x2.shape
    d_ff = w_down.shape[0]
    seq_tiles = seq // tm
    halo_per_tile = tm // SUBLANES
    n_halo = tokens // SUBLANES
    full = lambda shape: pl.BlockSpec(shape, lambda b, j: (0,) * len(shape), pipeline_mode=pl.Buffered(1))
    return pl.pallas_call(
        functools.partial(_ffn_kernel, d_ff=d_ff),
        grid=(tokens // seq, seq_tiles),
        in_specs=[
            pl.BlockSpec((tm, d_model), lambda b, j: (b * seq_tiles + j, 0)),
            pl.BlockSpec((SUBLANES, d_model),
                         lambda b, j: (jnp.maximum((b * seq_tiles + j) * halo_per_tile - 1, 0), 0)),
            pl.BlockSpec((SUBLANES, d_model),
                         lambda b, j: (jnp.minimum((b * seq_tiles + j + 1) * halo_per_tile, n_halo - 1), 0)),
            full(g.shape), full(w_up.shape), full(conv_w.shape), full(conv_b.shape), full(w_down.shape),
        ],
        out_specs=pl.BlockSpec((tm, d_model), lambda b, j: (b * seq_tiles + j, 0)),
        out_shape=jax.ShapeDtypeStruct((tokens, d_model), F32),
        scratch_shapes=[pltpu.VMEM((FFN_SLOTS, 2 * FFN_CHUNK // LANES, tm + 2 * SUBLANES, LANES), F32),
                        pltpu.VMEM((tm, d_ff), BF16)],
        compiler_params=_cparams("parallel", "parallel"),
        name="conv_ffn",
    )(x2, x2, x2, g, w_up, conv_w, conv_b, w_down)


def _rope_lane_tables(seq):
    half = ROPE_DIMS // 2
    pos = jnp.arange(seq, dtype=F32)
    inv = ROPE_THETA ** (-jnp.arange(0, ROPE_DIMS, 2, dtype=F32) / ROPE_DIMS)
    ang = pos[:, None] * inv[None, :]
    cos, sin = jnp.cos(ang), jnp.sin(ang)
    d = np.arange(LANES) % HEAD_DIM
    first, second = d < half, (d >= half) & (d < ROPE_DIMS)
    idx = d % half
    c_tab = jnp.where((first | second)[None, :], cos[:, idx], 1.0)
    a_tab = jnp.where(first[None, :], -sin[:, idx], 0.0)
    b_tab = jnp.where(second[None, :], sin[:, idx], 0.0)
    return jnp.stack([c_tab, a_tab, b_tab]).astype(F32)


def kernel(x, g_mix, w_in, gate_b, a_conv_w, a_conv_b, a_ln_g, a_ln_b, na_qn, na_kn, na_rpb, dil_qn, dil_kn, swa_qn, swa_kn, swa_sink, w_branch, w_out, g_ffn, w_up, ffn_conv_w, ffn_conv_b, w_down):
    bsz, seq, d_model = x.shape
    depth = w_in.shape[0]
    mix_w = d_model // N_BRANCH
    heads = mix_w // HEAD_DIM
    kv_heads = swa_sink.shape[1] // 2
    grp = heads // kv_heads
    n_gate = N_BRANCH * d_model
    assert mix_w % LANES == 0 and seq % (GRID_W * NA_ROWS) == 0
    assert all(w // (2 * d) == DIL_PAIRS[0][0] // 2 for w, d in DIL_PAIRS)

    rope_tab = _rope_lane_tables(seq)
    scale = HEAD_DIM ** -0.5 * LOG2_E
    tile_h = lambda v: jnp.tile(v.astype(F32), heads)
    x2 = x.reshape(bsz * seq, d_model)
    na_bias = _na_bias_table(na_rpb.astype(F32) * LOG2_E, seq // GRID_W)

    for l in range(depth):
        w_l = w_in[l]
        o_swk = w_l.shape[1] - 2 * kv_heads * HEAD_DIM
        rep_heads = lambda w: jnp.repeat(w.reshape(d_model, kv_heads, HEAD_DIM), grp, axis=1).reshape(d_model, mix_w)
        w_br = jnp.concatenate([
            w_l[:, n_gate:o_swk],
            rep_heads(w_l[:, o_swk:o_swk + kv_heads * HEAD_DIM]),
            rep_heads(w_l[:, o_swk + kv_heads * HEAD_DIM:]),
        ], axis=1).astype(BF16)
        gains = jnp.stack([tile_h(na_qn[l]) * scale, tile_h(na_kn[l]),
                           tile_h(dil_qn[l]) * scale, tile_h(dil_kn[l]),
                           tile_h(swa_qn[l]) * scale, tile_h(swa_kn[l]),
                           jnp.zeros((mix_w,), F32), jnp.zeros((mix_w,), F32)])

        a_u, na_q, na_k, na_v, dil_qkv, sw_q, sw_k, sw_v = _inproj(
            x2, g_mix[l][None, :], w_br, gains, rope_tab, seq=seq, tm=512)
        y_b = _na_attn(na_q, na_k, na_v, na_bias[l], seq=seq)
        y_c = _dilated_attn(dil_qkv, seq=seq, mix_w=mix_w)
        y_d = _swa_attn(swa_sink[l].astype(F32) * LOG2_E, sw_q, sw_k, sw_v, seq=seq)
        x2 = _merge(x2, g_mix[l][None, :], w_l[:, :n_gate].astype(BF16), gate_b[l].reshape(1, n_gate),
                    a_u, a_conv_w[l], a_conv_b[l][None, :], a_ln_g[l][None, :], a_ln_b[l][None, :],
                    (y_b, y_c, y_d), w_branch[l].astype(BF16), w_out[l].astype(BF16), seq=seq, tm=512)
        x2 = _ffn(x2, g_ffn[l][None, :], w_up[l].astype(BF16), ffn_conv_w[l], ffn_conv_b[l][None, :],
                  w_down[l].astype(BF16), seq=seq, tm=FFN_TM)
    return x2.reshape(bsz, seq, d_model)
```

```python
import functools

import numpy as np
import jax
import jax.numpy as jnp
from jax import lax
from jax.experimental import pallas as pl
from jax.experimental.pallas import tpu as pltpu

F32 = jnp.float32
BF16 = jnp.bfloat16

N_BRANCH = 4
HEAD_DIM = 64
CONV_A_K = 31
NA_ROWS = 8
NA_COLS = 16
GRID_W = 64
DIL_PAIRS = ((128, 1), (512, 4), (2048, 16))
SWA_WINDOW = 128
Q_BLOCK = 128
ROPE_THETA = 500000.0
ROPE_DIMS = HEAD_DIM // 4
FFN_CONV_K = 3
EPS = 1e-6
NEG_INF = -1e30
LOG2_E = 1.4426950408889634

LANES = 128
SUBLANES = 8
VMEM_LIMIT = 56 * 1024 * 1024

HEADS_PER_SLAB = LANES // HEAD_DIM


def _cparams(*sem):
    return pltpu.CompilerParams(dimension_semantics=sem, vmem_limit_bytes=VMEM_LIMIT)


def _low_head_mask():
    return lax.broadcasted_iota(jnp.int32, (1, LANES), 1) < HEAD_DIM


def _rmsnorm_rows(x, g):
    return x * lax.rsqrt(jnp.mean(x * x, axis=-1, keepdims=True) + EPS) * g


def _rope(y, rope_ref):
    half = ROPE_DIMS // 2
    return (y * rope_ref[0]
            + pltpu.roll(y, LANES - half, 1) * rope_ref[1]
            + pltpu.roll(y, half, 1) * rope_ref[2])


(SEG_GLU_A, SEG_GLU_G, SEG_NA_Q, SEG_NA_K, SEG_NA_V, SEG_DIL_Q, SEG_DIL_K, SEG_DIL_V,
 SEG_SW_Q, SEG_SW_K, SEG_SW_V) = range(11)
NORMED_SEGS = (SEG_NA_Q, SEG_NA_K, SEG_DIL_Q, SEG_DIL_K, SEG_SW_Q, SEG_SW_K)


def _inproj_kernel(x_ref, g_ref, w_ref, gains_ref, rope_ref,
                   a_ref, naq_ref, nak_ref, nav_ref, dil_ref, swq_ref, swk_ref, swv_ref, *, mix_w):
    n_slab = mix_w // LANES
    hb = _rmsnorm_rows(x_ref[...], g_ref[...]).astype(BF16)
    r = lax.broadcasted_iota(jnp.int32, (mix_w, mix_w), 0) // HEAD_DIM
    c = lax.broadcasted_iota(jnp.int32, (mix_w, mix_w), 1) // HEAD_DIM
    blockdiag = (r == c).astype(BF16)

    def proj(seg):
        return jnp.dot(hb, w_ref[:, seg * mix_w:(seg + 1) * mix_w], preferred_element_type=F32)

    def head_norm(row, v):
        ms = jnp.dot((v * v).astype(BF16), blockdiag, preferred_element_type=F32) * (1.0 / HEAD_DIM)
        return v * lax.rsqrt(ms + EPS) * gains_ref[row:row + 1, :]

    p, normed = {}, {}
    for row, seg in enumerate(NORMED_SEGS):
        p[seg] = proj(seg)
        if row:
            normed[NORMED_SEGS[row - 1]] = head_norm(row - 1, p[NORMED_SEGS[row - 1]])
    for seg in range(SEG_SW_V + 1):
        if seg not in p:
            p[seg] = proj(seg)
            if NORMED_SEGS[-1] not in normed:
                normed[NORMED_SEGS[-1]] = head_norm(len(NORMED_SEGS) - 1, p[NORMED_SEGS[-1]])

    def slabs(v):
        return [v[:, s * LANES:(s + 1) * LANES] for s in range(n_slab)]

    a_ref[...] = p[SEG_GLU_A] * jax.nn.sigmoid(p[SEG_GLU_G])
    naq_ref[...] = normed[SEG_NA_Q].astype(BF16)
    nak_ref[...] = normed[SEG_NA_K].astype(BF16)
    nav_ref[...] = p[SEG_NA_V].astype(BF16)
    for s in range(n_slab):
        dil_ref[s] = _rope(slabs(normed[SEG_DIL_Q])[s], rope_ref)
        dil_ref[n_slab + s] = _rope(slabs(normed[SEG_DIL_K])[s], rope_ref)
        dil_ref[2 * n_slab + s] = slabs(p[SEG_DIL_V])[s]
        lanes = slice(s * LANES, (s + 1) * LANES)
        swq_ref[:, lanes] = _rope(slabs(normed[SEG_SW_Q])[s], rope_ref).astype(BF16)
        swk_ref[:, lanes] = _rope(slabs(normed[SEG_SW_K])[s], rope_ref).astype(BF16)
    swv_ref[...] = p[SEG_SW_V].astype(BF16)


def _inproj(x2, g, w_br, gains, rope_tab, *, seq, tm):
    tokens, d_model = x2.shape
    n_br = w_br.shape[1]
    mix_w = gains.shape[1]
    n_slab = mix_w // LANES
    seq_tiles = seq // tm
    tok_spec = lambda w: pl.BlockSpec((tm, w), lambda i: (i, 0))
    full = lambda shape: pl.BlockSpec(shape, lambda i: (0,) * len(shape))
    out_shape = (
        jax.ShapeDtypeStruct((tokens, mix_w), F32),
        jax.ShapeDtypeStruct((tokens, mix_w), BF16),
        jax.ShapeDtypeStruct((tokens, mix_w), BF16),
        jax.ShapeDtypeStruct((tokens, mix_w), BF16),
        jax.ShapeDtypeStruct((3 * n_slab, tokens, LANES), F32),
        jax.ShapeDtypeStruct((tokens, mix_w), BF16),
        jax.ShapeDtypeStruct((tokens, mix_w), BF16),
        jax.ShapeDtypeStruct((tokens, mix_w), BF16),
    )
    out_specs = (
        tok_spec(mix_w), tok_spec(mix_w), tok_spec(mix_w), tok_spec(mix_w),
        pl.BlockSpec((3 * n_slab, tm, LANES), lambda i: (0, i, 0)),
        tok_spec(mix_w), tok_spec(mix_w), tok_spec(mix_w),
    )
    return pl.pallas_call(
        functools.partial(_inproj_kernel, mix_w=mix_w),
        grid=(tokens // tm,),
        in_specs=[
            tok_spec(d_model),
            full((1, d_model)),
            full((d_model, n_br)),
            full(gains.shape),
            pl.BlockSpec((3, tm, LANES), lambda i: (0, i % seq_tiles, 0)),
        ],
        out_specs=out_specs,
        out_shape=out_shape,
        compiler_params=_cparams("parallel"),
        name="inproj",
    )(x2, g, w_br, gains, rope_tab)


CONV_CHUNK = 128
CONV_PAD = 16


NA_GROUP = 4
DIL_GROUP = 4
SWA_GROUP = 2


def _stack_heads(q_slab, low):
    zero = jnp.zeros_like(q_slab)
    return jnp.concatenate([jnp.where(low, q_slab, zero), jnp.where(low, zero, q_slab)], axis=0)


def _unstack_heads(o, low):
    m = o.shape[0] // HEADS_PER_SLAB
    return jnp.where(low, o[:m], o[m:])


def _qk(q_stack, k_slab):
    return lax.dot_general(q_stack, k_slab, (((1,), (1,)), ((), ())), preferred_element_type=F32)


def _band_bias(nq, nk, shift, side):
    qi = lax.broadcasted_iota(jnp.int32, (HEADS_PER_SLAB * nq, nk), 0) % nq
    ki = lax.broadcasted_iota(jnp.int32, (HEADS_PER_SLAB * nq, nk), 1)
    return jnp.where(jnp.abs(qi + shift - ki) <= side, 0.0, NEG_INF).astype(F32)


def _fill_band_biases(mask_ref, nq, side):
    for c in range(mask_ref.shape[0]):
        mask_ref[c] = _band_bias(nq, mask_ref.shape[2], c * side, side)


def _na_kernel(q_ref, k_ref, v_ref, bias_ref, o_ref, *, rows, n_slab):
    low = _low_head_mask()
    kr = NA_ROWS
    n_edge_lo = kr // 2 + 1
    mid_hi = rows - (kr - kr // 2)

    def row_group(g, carry):
        chains = []
        for i in range(NA_GROUP):
            r = g * NA_GROUP + i
            row_start = jnp.clip(r - kr // 2, 0, rows - kr)
            cls = jnp.where(r < n_edge_lo, r, jnp.where(r <= mid_hi, n_edge_lo, r - mid_hi + n_edge_lo))
            q0 = pl.multiple_of(r * GRID_W, GRID_W)
            k0 = pl.multiple_of(row_start * GRID_W, GRID_W)
            for s in range(n_slab):
                chains.append((cls, s, pl.ds(q0, GRID_W), pl.ds(k0, kr * GRID_W),
                               slice(s * LANES, (s + 1) * LANES)))
        scores = [_qk(_stack_heads(q_ref[qrows, lanes], low), k_ref[krows, lanes]) + bias_ref[cls, s]
                  for cls, s, qrows, krows, lanes in chains]
        probs, dens = [], []
        for sc in scores:
            p = jnp.exp2(sc - jnp.max(sc, axis=-1, keepdims=True))
            dens.append(jnp.sum(p, axis=-1, keepdims=True))
            probs.append(p.astype(BF16))
        for (cls, s, qrows, krows, lanes), p, l in zip(chains, probs, dens):
            pv = jnp.dot(p, v_ref[krows, lanes], preferred_element_type=F32)
            o_ref[qrows, lanes] = _unstack_heads(pv / l, low).astype(o_ref.dtype)
        return carry

    lax.fori_loop(0, rows // NA_GROUP, row_group, 0)


def _na_bias_table(rpb, rows):
    kr, kc = NA_ROWS, NA_COLS
    depth, heads = rpb.shape[:2]
    n_edge_lo = kr // 2 + 1
    mid_hi = rows - (kr - kr // 2)
    rep = np.asarray(list(range(n_edge_lo)) + [n_edge_lo] + list(range(mid_hi + 1, rows)))
    row_start = np.clip(rep - kr // 2, 0, rows - kr)
    dr = row_start[:, None] + np.arange(kr)[None, :] - rep[:, None] + (NA_ROWS - 1)
    col = np.arange(GRID_W)
    col_start = np.clip(col - kc // 2, 0, GRID_W - kc)
    valid = (col[None, :] >= col_start[:, None]) & (col[None, :] < col_start[:, None] + kc)
    dc = col[None, :] - col[:, None] + (NA_COLS - 1)
    sel_r = (dr[:, :, None] == np.arange(2 * NA_ROWS - 1)).astype(np.float32)
    sel_c = ((dc[:, :, None] == np.arange(2 * NA_COLS - 1)) & valid[:, :, None]).astype(np.float32)
    tab = jnp.einsum('kir,lhrj,wcj->lkhwic', sel_r, rpb.astype(F32), sel_c,
                     precision=lax.Precision.HIGHEST)
    tab = tab + jnp.where(valid, 0.0, NEG_INF).astype(F32)[None, None, None, :, None, :]
    return tab.reshape(depth, len(rep), heads // HEADS_PER_SLAB, HEADS_PER_SLAB * GRID_W, kr * GRID_W)


def _na_attn(q, k, v, bias, *, seq):
    tokens, mix_w = q.shape
    seq_spec = pl.BlockSpec((seq, mix_w), lambda i: (i, 0))
    return pl.pallas_call(
        functools.partial(_na_kernel, rows=seq // GRID_W, n_slab=mix_w // LANES),
        grid=(tokens // seq,),
        in_specs=[seq_spec, seq_spec, seq_spec,
                  pl.BlockSpec(bias.shape, lambda i: (0, 0, 0, 0))],
        out_specs=seq_spec,
        out_shape=jax.ShapeDtypeStruct((tokens, mix_w), BF16),
        compiler_params=_cparams("parallel"),
        name="na_attn",
    )(q, k, v, bias)


def _dil_group_blocks(chains, low):
    scores = [_qk(_stack_heads(lq().astype(BF16), low), lk().astype(BF16)) + lb()
              for lq, lk, _, lb in chains]
    stats = []
    for sc in scores:
        m = jnp.max(sc, axis=-1, keepdims=True)
        p = jnp.exp2(sc - m)
        stats.append((p.astype(BF16), m, jnp.sum(p, axis=-1, keepdims=True)))
    outs = []
    for (_, _, lv, _), (p, m, l) in zip(chains, stats):
        acc = jnp.dot(p, lv().astype(BF16), preferred_element_type=F32)
        nq = acc.shape[0] // HEADS_PER_SLAB
        outs.append((_unstack_heads(acc, low), jnp.where(low, m[:nq], m[nq:]), jnp.where(low, l[:nq], l[nq:])))
    return outs


def _dilated_kernel(qkv_ref, o_ref, part2_ref, part3_ref, mask_ref, mask_sq_ref, *, n_slab):
    seq = qkv_ref.shape[1]
    low = _low_head_mask()
    (_, dil1), (_, dil2), (_, dil3) = DIL_PAIRS
    side = DIL_PAIRS[0][0] // (2 * dil1)
    qb = Q_BLOCK
    len2, len3 = seq // dil2, seq // dil3
    nblk2 = len2 // qb
    pitch3 = dil3 + 1
    _fill_band_biases(mask_ref, qb, side)
    _fill_band_biases(mask_sq_ref, len3, side)

    def chain(s, qrows, krows, bias):
        load = lambda kind, rows: (lambda: qkv_ref[kind * n_slab + s, rows, :])
        return (load(0, qrows), load(1, krows), load(2, krows), bias)

    def wide_bias(a0, k0):
        return lambda: mask_ref[(a0 - k0) // side]

    def g3(g, carry):
        blocks = [(s, pl.ds(g * DIL_GROUP + i, len3, stride=dil3))
                  for i in range(DIL_GROUP) for s in range(n_slab)]
        outs = _dil_group_blocks([chain(s, rows, rows, lambda: mask_sq_ref[0]) for s, rows in blocks], low)
        for i, parts in enumerate(outs):
            s = i % n_slab
            rows = pl.ds(g * DIL_GROUP + i // n_slab, len3, stride=pitch3)
            for kind, part in enumerate(parts):
                part3_ref[kind, s, rows, :] = part
        return carry

    lax.fori_loop(0, dil3 // DIL_GROUP, g3, 0)

    def g2(g, carry):
        blocks, chains = [], []
        for i in range(DIL_GROUP):
            t = g * DIL_GROUP + i
            b = t // nblk2
            a0 = (t % nblk2) * qb
            k0 = jnp.clip(a0 - side, 0, len2 - 2 * qb)
            qrows = pl.ds(b + dil2 * a0, qb, stride=dil2)
            krows = pl.ds(b + dil2 * k0, 2 * qb, stride=dil2)
            for s in range(n_slab):
                blocks.append((s, qrows))
                chains.append(chain(s, qrows, krows, wide_bias(a0, k0)))
        for (s, qrows), parts in zip(blocks, _dil_group_blocks(chains, low)):
            for kind, part in enumerate(parts):
                part2_ref[kind, s, qrows, :] = part
        return carry

    lax.fori_loop(0, dil2 * nblk2 // DIL_GROUP, g2, 0)

    def group3_rows(kind, s, a0):
        base = a0 + a0 // dil3
        return jnp.concatenate([part3_ref[kind, s, pl.ds(base + pitch3 * i, dil3, stride=1), :]
                                for i in range(qb // dil3)], axis=0)

    def g1(g, carry):
        blocks, chains = [], []
        for i in range(DIL_GROUP):
            a0 = pl.multiple_of((g * DIL_GROUP + i) * qb, qb)
            k0 = pl.multiple_of(jnp.clip(a0 - side, 0, seq - 2 * qb), side)
            for s in range(n_slab):
                blocks.append((s, a0))
                chains.append(chain(s, pl.ds(a0, qb), pl.ds(k0, 2 * qb), wide_bias(a0, k0)))
        for (s, a0), (acc1, m1, l1) in zip(blocks, _dil_group_blocks(chains, low)):
            qrows = pl.ds(a0, qb)
            m2, m3 = part2_ref[1, s, qrows, :], group3_rows(1, s, a0)
            m = jnp.maximum(jnp.maximum(m1, m2), m3)
            w1, w2, w3 = jnp.exp2(m1 - m), jnp.exp2(m2 - m), jnp.exp2(m3 - m)
            num = w1 * acc1 + w2 * part2_ref[0, s, qrows, :] + w3 * group3_rows(0, s, a0)
            den = w1 * l1 + w2 * part2_ref[2, s, qrows, :] + w3 * group3_rows(2, s, a0)
            o_ref[qrows, s * LANES:(s + 1) * LANES] = (num / den).astype(o_ref.dtype)
        return carry

    lax.fori_loop(0, seq // qb // DIL_GROUP, g1, 0)


def _dilated_attn(qkv, *, seq, mix_w):
    n3, tokens, _ = qkv.shape
    n_slab = n3 // 3
    return pl.pallas_call(
        functools.partial(_dilated_kernel, n_slab=n_slab),
        grid=(tokens // seq,),
        in_specs=[pl.BlockSpec((n3, seq, LANES), lambda i: (0, i, 0))],
        out_specs=pl.BlockSpec((seq, mix_w), lambda i: (i, 0)),
        out_shape=jax.ShapeDtypeStruct((tokens, mix_w), BF16),
        scratch_shapes=[pltpu.VMEM((3, n_slab, seq, LANES), F32),
                        pltpu.VMEM((3, n_slab, seq + seq // DIL_PAIRS[2][1], LANES), F32),
            pltpu.VMEM((3, HEADS_PER_SLAB * Q_BLOCK, 2 * Q_BLOCK), F32),
            pltpu.VMEM((1, HEADS_PER_SLAB * (seq // DIL_PAIRS[2][1]), seq // DIL_PAIRS[2][1]), F32)],
        compiler_params=_cparams("parallel"),
        name="dilated_attn",
    )(qkv)


def _swa_kernel(sink_ref, q_ref, k_ref, v_ref, o_ref, mask_ref, *, n_slab):
    seq = q_ref.shape[0]
    low = _low_head_mask()
    qb = Q_BLOCK
    nk = qb + 2 * SWA_WINDOW
    _fill_band_biases(mask_ref, qb, SWA_WINDOW)
    first_head = lax.broadcasted_iota(jnp.int32, (HEADS_PER_SLAB * qb, 1), 0) < qb

    def block_group(g, carry):
        chains = []
        for i in range(SWA_GROUP):
            a0 = pl.multiple_of((g * SWA_GROUP + i) * qb, qb)
            k0 = pl.multiple_of(jnp.clip(a0 - SWA_WINDOW, 0, seq - nk), qb)
            cls = (a0 - k0) // SWA_WINDOW
            for s in range(n_slab):
                chains.append((s, pl.ds(a0, qb), pl.ds(k0, nk), slice(s * LANES, (s + 1) * LANES), cls))
        scores = [_qk(_stack_heads(q_ref[qrows, lanes], low), k_ref[krows, lanes]) + mask_ref[cls]
                  for s, qrows, krows, lanes, cls in chains]
        probs, dens = [], []
        for (s, _, _, _, _), sc in zip(chains, scores):
            sink = jnp.where(first_head, sink_ref[s * HEADS_PER_SLAB], sink_ref[s * HEADS_PER_SLAB + 1])
            m = jnp.maximum(jnp.max(sc, axis=-1, keepdims=True), sink)
            p = jnp.exp2(sc - m)
            dens.append(jnp.sum(p, axis=-1, keepdims=True) + jnp.exp2(sink - m))
            probs.append(p.astype(BF16))
        for (s, qrows, krows, lanes, _), p, denom in zip(chains, probs, dens):
            pv = jnp.dot(p, v_ref[krows, lanes], preferred_element_type=F32)
            o_ref[qrows, lanes] = _unstack_heads(pv / denom, low).astype(o_ref.dtype)
        return carry

    lax.fori_loop(0, seq // qb // SWA_GROUP, block_group, 0)


def _swa_attn(sink, q, k, v, *, seq):
    tokens, mix_w = q.shape
    seq_spec = pl.BlockSpec((seq, mix_w), lambda i: (i, 0))
    return pl.pallas_call(
        functools.partial(_swa_kernel, n_slab=mix_w // LANES),
        grid=(tokens // seq,),
        in_specs=[pl.BlockSpec(memory_space=pltpu.SMEM), seq_spec, seq_spec, seq_spec],
        out_specs=seq_spec,
        out_shape=jax.ShapeDtypeStruct((tokens, mix_w), BF16),
        scratch_shapes=[pltpu.VMEM((3, HEADS_PER_SLAB * Q_BLOCK, Q_BLOCK + 2 * SWA_WINDOW), F32)],
        compiler_params=_cparams("parallel"),
        name="swa_attn",
    )(sink, q, k, v)


def _conformer_rows(pad_ref, ya_ref, cw_ref, cb_ref, lg_ref, lb_ref, row_blocks):
    n_slab = pad_ref.shape[0]
    ch = n_slab * LANES
    half = (CONV_A_K - 1) // 2
    for c0 in row_blocks:
        accs = []
        for s in range(n_slab):
            lanes = slice(s * LANES, (s + 1) * LANES)
            acc = cb_ref[:, lanes]
            for k in range(CONV_A_K):
                r0 = c0 + CONV_PAD - half + k
                acc = acc + pad_ref[s, r0:r0 + CONV_CHUNK, :] * cw_ref[k:k + 1, lanes]
            accs.append(acc)
        mu = sum(jnp.sum(a, axis=-1, keepdims=True) for a in accs) * (1.0 / ch)
        ds = [a - mu for a in accs]
        var = sum(jnp.sum(d * d, axis=-1, keepdims=True) for d in ds) * (1.0 / ch)
        inv = lax.rsqrt(var + EPS)
        for s, d in enumerate(ds):
            lanes = slice(s * LANES, (s + 1) * LANES)
            y = d * inv * lg_ref[:, lanes] + lb_ref[:, lanes]
            ya_ref[c0:c0 + CONV_CHUNK, lanes] = (y * jax.nn.sigmoid(y)).astype(ya_ref.dtype)


def _merge_kernel(x_ref, g_ref, wg_ref, bg_ref, au_ref, aup_ref, aun_ref, cw_ref, cb_ref, lg_ref, lb_ref,
                  yb_ref, yc_ref, yd_ref, wb_ref, wo_ref, o_ref, pad_ref, ya_ref, *, seq_tiles):
    tm, d_model = x_ref.shape
    j = pl.program_id(0) % seq_tiles
    for s in range(pad_ref.shape[0]):
        lanes = slice(s * LANES, (s + 1) * LANES)
        pad_ref[s, 0:CONV_PAD, :] = jnp.where(j > 0, aup_ref[:, lanes], 0.0)
        pad_ref[s, CONV_PAD:CONV_PAD + tm, :] = au_ref[:, lanes]
        pad_ref[s, CONV_PAD + tm:2 * CONV_PAD + tm, :] = jnp.where(j < seq_tiles - 1, aun_ref[:, lanes], 0.0)

    x = x_ref[...]
    hb = _rmsnorm_rows(x, g_ref[...]).astype(BF16)

    def gate(n):
        cols = slice(n * d_model, (n + 1) * d_model)
        return jax.nn.sigmoid(jnp.dot(hb, wg_ref[:, cols], preferred_element_type=F32) + bg_ref[:, cols])

    def branch(n, y):
        return jnp.dot(y, wb_ref[n], preferred_element_type=F32)

    blocks = list(range(0, tm, CONV_CHUNK))
    others = [(1, yb_ref), (2, yc_ref), (3, yd_ref)]
    gates = {0: gate(0)}
    big = ([lambda n=n: gates.__setitem__(n, gate(n)) for n, _ in others]
           + [lambda n=n, y_ref=y_ref: gates[n] * branch(n, y_ref[...]) for n, y_ref in others])
    a_terms, other_terms = [], []
    for i in range(max(len(blocks), len(big))):
        if i < len(big):
            out = big[i]()
            if out is not None:
                other_terms.append(out)
        if i < len(blocks):
            c0 = blocks[i]
            _conformer_rows(pad_ref, ya_ref, cw_ref, cb_ref, lg_ref, lb_ref, [c0])
            a_terms.append(gates[0][c0:c0 + CONV_CHUNK] * branch(0, ya_ref[c0:c0 + CONV_CHUNK, :]))
    mixed = jnp.concatenate(a_terms, axis=0)
    for t in other_terms:
        mixed = mixed + t
    o_ref[...] = x + jnp.dot(mixed.astype(BF16), wo_ref[...], preferred_element_type=F32)


def _merge(x2, g, w_gate, b_gate, a_u, conv_w, conv_b, ln_g, ln_b, ys, w_branch, w_out, *, seq, tm):
    tokens, d_model = x2.shape
    mix_w = a_u.shape[1]
    seq_tiles = seq // tm
    halo_per_tile = tm // CONV_PAD
    n_halo = tokens // CONV_PAD
    tok_spec = lambda w: pl.BlockSpec((tm, w), lambda i: (i, 0))
    full = lambda shape: pl.BlockSpec(shape, lambda i: (0,) * len(shape))
    return pl.pallas_call(
        functools.partial(_merge_kernel, seq_tiles=seq_tiles),
        grid=(tokens // tm,),
        in_specs=[tok_spec(d_model), full(g.shape), full(w_gate.shape), full(b_gate.shape),
                  tok_spec(mix_w),
                  pl.BlockSpec((CONV_PAD, mix_w), lambda i: (jnp.maximum(i * halo_per_tile - 1, 0), 0)),
                  pl.BlockSpec((CONV_PAD, mix_w), lambda i: (jnp.minimum((i + 1) * halo_per_tile, n_halo - 1), 0)),
                  full(conv_w.shape), full(conv_b.shape), full(ln_g.shape), full(ln_b.shape)]
                 + [tok_spec(mix_w)] * len(ys)
                 + [full(w_branch.shape), full(w_out.shape)],
        out_specs=tok_spec(d_model),
        out_shape=jax.ShapeDtypeStruct((tokens, d_model), F32),
        scratch_shapes=[pltpu.VMEM((mix_w // LANES, tm + 2 * CONV_PAD, LANES), F32),
                        pltpu.VMEM((tm, mix_w), BF16)],
        compiler_params=_cparams("parallel"),
        name="merge",
    )(x2, g, w_gate, b_gate, a_u, a_u, a_u, conv_w, conv_b, ln_g, ln_b, *ys, w_branch, w_out)


FFN_CHUNK = 256


FFN_SLOTS = 2
FFN_TM = 512
FFN_DOWN_GROUP = 2


def _down_groups(n_chunk):
    groups, left = [], n_chunk
    size = FFN_DOWN_GROUP
    while left:
        while size > 1 and left < 2 * size:
            size //= 2
        groups.append(size)
        left -= size
    return groups


def _ffn_kernel(x_ref, xp_ref, xn_ref, g_ref, wu_ref, cw_ref, cb_ref, wd_ref, o_ref, u_ref, act_ref, *, d_ff):
    j = pl.program_id(1)
    tm = x_ref.shape[0]
    g = g_ref[...]
    hp = jnp.where(j > 0, _rmsnorm_rows(xp_ref[...], g), 0.0)
    hn = jnp.where(j < pl.num_programs(1) - 1, _rmsnorm_rows(xn_ref[...], g), 0.0)
    hcat = jnp.concatenate([hp, _rmsnorm_rows(x_ref[...], g), hn], axis=0).astype(BF16)

    n_chunk = d_ff // FFN_CHUNK
    n_slab = FFN_CHUNK // LANES

    def col0(c, half):
        return half * d_ff + c * FFN_CHUNK

    def up_project(c):
        for half in range(2):
            u = jnp.dot(hcat, wu_ref[:, col0(c, half):col0(c, half) + FFN_CHUNK], preferred_element_type=F32)
            for s in range(n_slab):
                u_ref[c % FFN_SLOTS, half * n_slab + s] = u[:, s * LANES:(s + 1) * LANES]

    def conv_act(c):
        def conv(half, s):
            lanes = slice(col0(c, half) + s * LANES, col0(c, half) + (s + 1) * LANES)
            y = cb_ref[:, lanes]
            for k in range(FFN_CONV_K):
                rows = slice(SUBLANES - 1 + k, SUBLANES - 1 + k + tm)
                y = y + u_ref[c % FFN_SLOTS, half * n_slab + s, rows, :] * cw_ref[k:k + 1, lanes]
            return y

        for s in range(n_slab):
            gate, up = conv(0, s), conv(1, s)
            out_lanes = slice(c * FFN_CHUNK + s * LANES, c * FFN_CHUNK + (s + 1) * LANES)
            act_ref[:, out_lanes] = (gate * jax.nn.sigmoid(gate) * up).astype(BF16)

    out = x_ref[...]
    group_start = 0
    group_ends = set(np.cumsum(_down_groups(n_chunk)).tolist())
    up_project(0)
    for c in range(n_chunk):
        if c + 1 < n_chunk:
            up_project(c + 1)
        conv_act(c)
        if c + 1 in group_ends:
            rows = slice(group_start * FFN_CHUNK, (c + 1) * FFN_CHUNK)
            out = out + jnp.dot(act_ref[:, rows], wd_ref[rows, :], preferred_element_type=F32)
            group_start = c + 1
    o_ref[...] = out


def _ffn(x2, g, w_up, conv_w, conv_b, w_down, *, seq, tm):
    tokens, d_model = x2.shape
    d_ff = w_down.shape[0]
    seq_tiles = seq // tm
    halo_per_tile = tm // SUBLANES
    n_halo = tokens // SUBLANES
    full = lambda shape: pl.BlockSpec(shape, lambda b, j: (0,) * len(shape), pipeline_mode=pl.Buffered(1))
    return pl.pallas_call(
        functools.partial(_ffn_kernel, d_ff=d_ff),
        grid=(tokens // seq, seq_tiles),
        in_specs=[
            pl.BlockSpec((tm, d_model), lambda b, j: (b * seq_tiles + j, 0)),
            pl.BlockSpec((SUBLANES, d_model),
                         lambda b, j: (jnp.maximum((b * seq_tiles + j) * halo_per_tile - 1, 0), 0)),
            pl.BlockSpec((SUBLANES, d_model),
                         lambda b, j: (jnp.minimum((b * seq_tiles + j + 1) * halo_per_tile, n_halo - 1), 0)),
            full(g.shape), full(w_up.shape), full(conv_w.shape), full(conv_b.shape), full(w_down.shape),
        ],
        out_specs=pl.BlockSpec((tm, d_model), lambda b, j: (b * seq_tiles + j, 0)),
        out_shape=jax.ShapeDtypeStruct((tokens, d_model), F32),
        scratch_shapes=[pltpu.VMEM((FFN_SLOTS, 2 * FFN_CHUNK // LANES, tm + 2 * SUBLANES, LANES), F32),
                        pltpu.VMEM((tm, d_ff), BF16)],
        compiler_params=_cparams("parallel", "parallel"),
        name="conv_ffn",
    )(x2, x2, x2, g, w_up, conv_w, conv_b, w_down)


def _rope_lane_tables(seq):
    half = ROPE_DIMS // 2
    pos = jnp.arange(seq, dtype=F32)
    inv = ROPE_THETA ** (-jnp.arange(0, ROPE_DIMS, 2, dtype=F32) / ROPE_DIMS)
    ang = pos[:, None] * inv[None, :]
    cos, sin = jnp.cos(ang), jnp.sin(ang)
    d = np.arange(LANES) % HEAD_DIM
    first, second = d < half, (d >= half) & (d < ROPE_DIMS)
    idx = d % half
    c_tab = jnp.where((first | second)[None, :], cos[:, idx], 1.0)
    a_tab = jnp.where(first[None, :], -sin[:, idx], 0.0)
    b_tab = jnp.where(second[None, :], sin[:, idx], 0.0)
    return jnp.stack([c_tab, a_tab, b_tab]).astype(F32)


def kernel(x, g_mix, w_in, gate_b, a_conv_w, a_conv_b, a_ln_g, a_ln_b, na_qn, na_kn, na_rpb, dil_qn, dil_kn, swa_qn, swa_kn, swa_sink, w_branch, w_out, g_ffn, w_up, ffn_conv_w, ffn_conv_b, w_down):
    bsz, seq, d_model = x.shape
    depth = w_in.shape[0]
    mix_w = d_model // N_BRANCH
    heads = mix_w // HEAD_DIM
    kv_heads = swa_sink.shape[1] // 2
    grp = heads // kv_heads
    n_gate = N_BRANCH * d_model
    assert mix_w % LANES == 0 and seq % (GRID_W * NA_ROWS) == 0
    assert all(w // (2 * d) == DIL_PAIRS[0][0] // 2 for w, d in DIL_PAIRS)

    rope_tab = _rope_lane_tables(seq)
    scale = HEAD_DIM ** -0.5 * LOG2_E
    tile_h = lambda v: jnp.tile(v.astype(F32), heads)
    x2 = x.reshape(bsz * seq, d_model)
    na_bias = _na_bias_table(na_rpb.astype(F32) * LOG2_E, seq // GRID_W)

    for l in range(depth):
        w_l = w_in[l]
        o_swk = w_l.shape[1] - 2 * kv_heads * HEAD_DIM
        rep_heads = lambda w: jnp.repeat(w.reshape(d_model, kv_heads, HEAD_DIM), grp, axis=1).reshape(d_model, mix_w)
        w_br = jnp.concatenate([
            w_l[:, n_gate:o_swk],
            rep_heads(w_l[:, o_swk:o_swk + kv_heads * HEAD_DIM]),
            rep_heads(w_l[:, o_swk + kv_heads * HEAD_DIM:]),
        ], axis=1).astype(BF16)
        gains = jnp.stack([tile_h(na_qn[l]) * scale, tile_h(na_kn[l]),
                           tile_h(dil_qn[l]) * scale, tile_h(dil_kn[l]),
                           tile_h(swa_qn[l]) * scale, tile_h(swa_kn[l]),
                           jnp.zeros((mix_w,), F32), jnp.zeros((mix_w,), F32)])

        a_u, na_q, na_k, na_v, dil_qkv, sw_q, sw_k, sw_v = _inproj(
            x2, g_mix[l][None, :], w_br, gains, rope_tab, seq=seq, tm=512)
        y_b = _na_attn(na_q, na_k, na_v, na_bias[l], seq=seq)
        y_c = _dilated_attn(dil_qkv, seq=seq, mix_w=mix_w)
        y_d = _swa_attn(swa_sink[l].astype(F32) * LOG2_E, sw_q, sw_k, sw_v, seq=seq)
        x2 = _merge(x2, g_mix[l][None, :], w_l[:, :n_gate].astype(BF16), gate_b[l].reshape(1, n_gate),
                    a_u, a_conv_w[l], a_conv_b[l][None, :], a_ln_g[l][None, :], a_ln_b[l][None, :],
                    (y_b, y_c, y_d), w_branch[l].astype(BF16), w_out[l].astype(BF16), seq=seq, tm=512)
        x2 = _ffn(x2, g_ffn[l][None, :], w_up[l].astype(BF16), ffn_conv_w[l], ffn_conv_b[l][None, :],
                  w_down[l].astype(BF16), seq=seq, tm=FFN_TM)
    return x2.reshape(bsz, seq, d_model)
```

```python
import functools

import numpy as np
import jax
import jax.numpy as jnp
from jax import lax
from jax.experimental import pallas as pl
from jax.experimental.pallas import tpu as pltpu

F32 = jnp.float32
BF16 = jnp.bfloat16

N_BRANCH = 4
HEAD_DIM = 64
CONV_A_K = 31
NA_ROWS = 8
NA_COLS = 16
GRID_W = 64
DIL_PAIRS = ((128, 1), (512, 4), (2048, 16))
SWA_WINDOW = 128
Q_BLOCK = 128
ROPE_THETA = 500000.0
ROPE_DIMS = HEAD_DIM // 4
FFN_CONV_K = 3
EPS = 1e-6
NEG_INF = -1e30
LOG2_E = 1.4426950408889634

LANES = 128
SUBLANES = 8
VMEM_LIMIT = 56 * 1024 * 1024

HEADS_PER_SLAB = LANES // HEAD_DIM


def _cparams(*sem):
    return pltpu.CompilerParams(dimension_semantics=sem, vmem_limit_bytes=VMEM_LIMIT)


def _low_head_mask():
    return lax.broadcasted_iota(jnp.int32, (1, LANES), 1) < HEAD_DIM


def _rmsnorm_rows(x, g):
    return x * lax.rsqrt(jnp.mean(x * x, axis=-1, keepdims=True) + EPS) * g


def _rope(y, rope_ref):
    half = ROPE_DIMS // 2
    return (y * rope_ref[0]
            + pltpu.roll(y, LANES - half, 1) * rope_ref[1]
            + pltpu.roll(y, half, 1) * rope_ref[2])


(SEG_GLU_A, SEG_GLU_G, SEG_NA_Q, SEG_NA_K, SEG_NA_V, SEG_DIL_Q, SEG_DIL_K, SEG_DIL_V,
 SEG_SW_Q, SEG_SW_K, SEG_SW_V) = range(11)
NORMED_SEGS = (SEG_NA_Q, SEG_NA_K, SEG_DIL_Q, SEG_DIL_K, SEG_SW_Q, SEG_SW_K)


def _inproj_kernel(x_ref, g_ref, w_ref, gains_ref, rope_ref,
                   a_ref, naq_ref, nak_ref, nav_ref, dil_ref, swq_ref, swk_ref, swv_ref, *, mix_w):
    n_slab = mix_w // LANES
    hb = _rmsnorm_rows(x_ref[...], g_ref[...]).astype(BF16)
    r = lax.broadcasted_iota(jnp.int32, (mix_w, mix_w), 0) // HEAD_DIM
    c = lax.broadcasted_iota(jnp.int32, (mix_w, mix_w), 1) // HEAD_DIM
    blockdiag = (r == c).astype(BF16)

    def proj(seg):
        return jnp.dot(hb, w_ref[:, seg * mix_w:(seg + 1) * mix_w], preferred_element_type=F32)

    def head_norm(row, v):
        ms = jnp.dot((v * v).astype(BF16), blockdiag, preferred_element_type=F32) * (1.0 / HEAD_DIM)
        return v * lax.rsqrt(ms + EPS) * gains_ref[row:row + 1, :]

    p, normed = {}, {}
    for row, seg in enumerate(NORMED_SEGS):
        p[seg] = proj(seg)
        if row:
            normed[NORMED_SEGS[row - 1]] = head_norm(row - 1, p[NORMED_SEGS[row - 1]])
    for seg in range(SEG_SW_V + 1):
        if seg not in p:
            p[seg] = proj(seg)
            if NORMED_SEGS[-1] not in normed:
                normed[NORMED_SEGS[-1]] = head_norm(len(NORMED_SEGS) - 1, p[NORMED_SEGS[-1]])

    def slabs(v):
        return [v[:, s * LANES:(s + 1) * LANES] for s in range(n_slab)]

    a_ref[...] = p[SEG_GLU_A] * jax.nn.sigmoid(p[SEG_GLU_G])
    naq_ref[...] = normed[SEG_NA_Q].astype(BF16)
    nak_ref[...] = normed[SEG_NA_K].astype(BF16)
    nav_ref[...] = p[SEG_NA_V].astype(BF16)
    for s in range(n_slab):
        dil_ref[s] = _rope(slabs(normed[SEG_DIL_Q])[s], rope_ref)
        dil_ref[n_slab + s] = _rope(slabs(normed[SEG_DIL_K])[s], rope_ref)
        dil_ref[2 * n_slab + s] = slabs(p[SEG_DIL_V])[s]
        lanes = slice(s * LANES, (s + 1) * LANES)
        swq_ref[:, lanes] = _rope(slabs(normed[SEG_SW_Q])[s], rope_ref).astype(BF16)
        swk_ref[:, lanes] = _rope(slabs(normed[SEG_SW_K])[s], rope_ref).astype(BF16)
    swv_ref[...] = p[SEG_SW_V].astype(BF16)


def _inproj(x2, g, w_br, gains, rope_tab, *, seq, tm):
    tokens, d_model = x2.shape
    n_br = w_br.shape[1]
    mix_w = gains.shape[1]
    n_slab = mix_w // LANES
    seq_tiles = seq // tm
    tok_spec = lambda w: pl.BlockSpec((tm, w), lambda i: (i, 0))
    full = lambda shape: pl.BlockSpec(shape, lambda i: (0,) * len(shape))
    out_shape = (
        jax.ShapeDtypeStruct((tokens, mix_w), F32),
        jax.ShapeDtypeStruct((tokens, mix_w), BF16),
        jax.ShapeDtypeStruct((tokens, mix_w), BF16),
        jax.ShapeDtypeStruct((tokens, mix_w), BF16),
        jax.ShapeDtypeStruct((3 * n_slab, tokens, LANES), F32),
        jax.ShapeDtypeStruct((tokens, mix_w), BF16),
        jax.ShapeDtypeStruct((tokens, mix_w), BF16),
        jax.ShapeDtypeStruct((tokens, mix_w), BF16),
    )
    out_specs = (
        tok_spec(mix_w), tok_spec(mix_w), tok_spec(mix_w), tok_spec(mix_w),
        pl.BlockSpec((3 * n_slab, tm, LANES), lambda i: (0, i, 0)),
        tok_spec(mix_w), tok_spec(mix_w), tok_spec(mix_w),
    )
    return pl.pallas_call(
        functools.partial(_inproj_kernel, mix_w=mix_w),
        grid=(tokens // tm,),
        in_specs=[
            tok_spec(d_model),
            full((1, d_model)),
            full((d_model, n_br)),
            full(gains.shape),
            pl.BlockSpec((3, tm, LANES), lambda i: (0, i % seq_tiles, 0)),
        ],
        out_specs=out_specs,
        out_shape=out_shape,
        compiler_params=_cparams("parallel"),
        name="inproj",
    )(x2, g, w_br, gains, rope_tab)


CONV_CHUNK = 128
CONV_PAD = 16


NA_GROUP = 8
DIL_GROUP = 8
SWA_GROUP = 8


def _stack_heads(q_slab, low):
    zero = jnp.zeros_like(q_slab)
    return jnp.concatenate([jnp.where(low, q_slab, zero), jnp.where(low, zero, q_slab)], axis=0)


def _unstack_heads(o, low):
    m = o.shape[0] // HEADS_PER_SLAB
    return jnp.where(low, o[:m], o[m:])


def _qk(q_stack, k_slab):
    return lax.dot_general(q_stack, k_slab, (((1,), (1,)), ((), ())), preferred_element_type=F32)


ATTN_AHEAD = 2
NA_AHEAD = 4


def _pipeline_chains(chains, score, finish, ahead=ATTN_AHEAD):
    pending, outs = [], []
    for chain in chains:
        pending.append((chain, score(chain)))
        if len(pending) > ahead:
            outs.append(finish(*pending.pop(0)))
    while pending:
        outs.append(finish(*pending.pop(0)))
    return outs


def _band_bias(nq, nk, shift, side):
    qi = lax.broadcasted_iota(jnp.int32, (HEADS_PER_SLAB * nq, nk), 0) % nq
    ki = lax.broadcasted_iota(jnp.int32, (HEADS_PER_SLAB * nq, nk), 1)
    return jnp.where(jnp.abs(qi + shift - ki) <= side, 0.0, NEG_INF).astype(F32)


def _fill_band_biases(mask_ref, nq, side):
    for c in range(mask_ref.shape[0]):
        mask_ref[c] = _band_bias(nq, mask_ref.shape[2], c * side, side)


def _na_kernel(q_ref, k_ref, v_ref, bias_ref, o_ref, *, rows, n_slab):
    low = _low_head_mask()
    kr = NA_ROWS
    n_edge_lo = kr // 2 + 1
    mid_hi = rows - (kr - kr // 2)

    def row_group(g, carry):
        chains = []
        for i in range(NA_GROUP):
            r = g * NA_GROUP + i
            row_start = jnp.clip(r - kr // 2, 0, rows - kr)
            cls = jnp.where(r < n_edge_lo, r, jnp.where(r <= mid_hi, n_edge_lo, r - mid_hi + n_edge_lo))
            q0 = pl.multiple_of(r * GRID_W, GRID_W)
            k0 = pl.multiple_of(row_start * GRID_W, GRID_W)
            for s in range(n_slab):
                chains.append((cls, s, pl.ds(q0, GRID_W), pl.ds(k0, kr * GRID_W),
                               slice(s * LANES, (s + 1) * LANES)))
        def score(chain):
            cls, s, qrows, krows, lanes = chain
            return _qk(_stack_heads(q_ref[qrows, lanes], low), k_ref[krows, lanes]) + bias_ref[cls, s]

        def finish(chain, sc):
            _, _, qrows, krows, lanes = chain
            p = jnp.exp2(sc - jnp.max(sc, axis=-1, keepdims=True))
            l = jnp.sum(p, axis=-1, keepdims=True)
            pv = jnp.dot(p.astype(BF16), v_ref[krows, lanes], preferred_element_type=F32)
            o_ref[qrows, lanes] = _unstack_heads(pv / l, low).astype(o_ref.dtype)

        _pipeline_chains(chains, score, finish, ahead=NA_AHEAD)
        return carry

    lax.fori_loop(0, rows // NA_GROUP, row_group, 0)


def _na_bias_table(rpb, rows):
    kr, kc = NA_ROWS, NA_COLS
    depth, heads = rpb.shape[:2]
    n_edge_lo = kr // 2 + 1
    mid_hi = rows - (kr - kr // 2)
    rep = np.asarray(list(range(n_edge_lo)) + [n_edge_lo] + list(range(mid_hi + 1, rows)))
    row_start = np.clip(rep - kr // 2, 0, rows - kr)
    dr = row_start[:, None] + np.arange(kr)[None, :] - rep[:, None] + (NA_ROWS - 1)
    col = np.arange(GRID_W)
    col_start = np.clip(col - kc // 2, 0, GRID_W - kc)
    valid = (col[None, :] >= col_start[:, None]) & (col[None, :] < col_start[:, None] + kc)
    dc = col[None, :] - col[:, None] + (NA_COLS - 1)
    sel_r = (dr[:, :, None] == np.arange(2 * NA_ROWS - 1)).astype(np.float32)
    sel_c = ((dc[:, :, None] == np.arange(2 * NA_COLS - 1)) & valid[:, :, None]).astype(np.float32)
    tab = jnp.einsum('kir,lhrj,wcj->lkhwic', sel_r, rpb.astype(F32), sel_c,
                     precision=lax.Precision.HIGHEST)
    tab = tab + jnp.where(valid, 0.0, NEG_INF).astype(F32)[None, None, None, :, None, :]
    return tab.reshape(depth, len(rep), heads // HEADS_PER_SLAB, HEADS_PER_SLAB * GRID_W, kr * GRID_W)


def _na_attn(q, k, v, bias, *, seq):
    tokens, mix_w = q.shape
    seq_spec = pl.BlockSpec((seq, mix_w), lambda i: (i, 0))
    return pl.pallas_call(
        functools.partial(_na_kernel, rows=seq // GRID_W, n_slab=mix_w // LANES),
        grid=(tokens // seq,),
        in_specs=[seq_spec, seq_spec, seq_spec,
                  pl.BlockSpec(bias.shape, lambda i: (0, 0, 0, 0))],
        out_specs=seq_spec,
        out_shape=jax.ShapeDtypeStruct((tokens, mix_w), BF16),
        compiler_params=_cparams("parallel"),
        name="na_attn",
    )(q, k, v, bias)


def _dil_group_blocks(chains, low):
    def score(chain):
        lq, lk, _, lb = chain
        return _qk(_stack_heads(lq().astype(BF16), low), lk().astype(BF16)) + lb()

    def finish(chain, sc):
        m = jnp.max(sc, axis=-1, keepdims=True)
        p = jnp.exp2(sc - m)
        l = jnp.sum(p, axis=-1, keepdims=True)
        acc = jnp.dot(p.astype(BF16), chain[2]().astype(BF16), preferred_element_type=F32)
        nq = acc.shape[0] // HEADS_PER_SLAB
        return (_unstack_heads(acc, low), jnp.where(low, m[:nq], m[nq:]), jnp.where(low, l[:nq], l[nq:]))

    return _pipeline_chains(chains, score, finish)


def _dilated_kernel(qkv_ref, o_ref, part2_ref, part3_ref, mask_ref, mask_sq_ref, *, n_slab):
    seq = qkv_ref.shape[1]
    low = _low_head_mask()
    (_, dil1), (_, dil2), (_, dil3) = DIL_PAIRS
    side = DIL_PAIRS[0][0] // (2 * dil1)
    qb = Q_BLOCK
    len2, len3 = seq // dil2, seq // dil3
    nblk2 = len2 // qb
    pitch3 = dil3 + 1
    _fill_band_biases(mask_ref, qb, side)
    _fill_band_biases(mask_sq_ref, len3, side)

    def chain(s, qrows, krows, bias):
        load = lambda kind, rows: (lambda: qkv_ref[kind * n_slab + s, rows, :])
        return (load(0, qrows), load(1, krows), load(2, krows), bias)

    def wide_bias(a0, k0):
        return lambda: mask_ref[(a0 - k0) // side]

    def g3(g, carry):
        blocks = [(s, pl.ds(g * DIL_GROUP + i, len3, stride=dil3))
                  for i in range(DIL_GROUP) for s in range(n_slab)]
        outs = _dil_group_blocks([chain(s, rows, rows, lambda: mask_sq_ref[0]) for s, rows in blocks], low)
        for i, parts in enumerate(outs):
            s = i % n_slab
            rows = pl.ds(g * DIL_GROUP + i // n_slab, len3, stride=pitch3)
            for kind, part in enumerate(parts):
                part3_ref[kind, s, rows, :] = part
        return carry

    lax.fori_loop(0, dil3 // DIL_GROUP, g3, 0)

    def g2(g, carry):
        blocks, chains = [], []
        for i in range(DIL_GROUP):
            t = g * DIL_GROUP + i
            b = t // nblk2
            a0 = (t % nblk2) * qb
            k0 = jnp.clip(a0 - side, 0, len2 - 2 * qb)
            qrows = pl.ds(b + dil2 * a0, qb, stride=dil2)
            krows = pl.ds(b + dil2 * k0, 2 * qb, stride=dil2)
            for s in range(n_slab):
                blocks.append((s, qrows))
                chains.append(chain(s, qrows, krows, wide_bias(a0, k0)))
        for (s, qrows), parts in zip(blocks, _dil_group_blocks(chains, low)):
            for kind, part in enumerate(parts):
                part2_ref[kind, s, qrows, :] = part
        return carry

    lax.fori_loop(0, dil2 * nblk2 // DIL_GROUP, g2, 0)

    def group3_rows(kind, s, a0):
        base = a0 + a0 // dil3
        return jnp.concatenate([part3_ref[kind, s, pl.ds(base + pitch3 * i, dil3, stride=1), :]
                                for i in range(qb // dil3)], axis=0)

    def g1(g, carry):
        blocks, chains = [], []
        for i in range(DIL_GROUP):
            a0 = pl.multiple_of((g * DIL_GROUP + i) * qb, qb)
            k0 = pl.multiple_of(jnp.clip(a0 - side, 0, seq - 2 * qb), side)
            for s in range(n_slab):
                blocks.append((s, a0))
                chains.append(chain(s, pl.ds(a0, qb), pl.ds(k0, 2 * qb), wide_bias(a0, k0)))
        for (s, a0), (acc1, m1, l1) in zip(blocks, _dil_group_blocks(chains, low)):
            qrows = pl.ds(a0, qb)
            m2, m3 = part2_ref[1, s, qrows, :], group3_rows(1, s, a0)
            m = jnp.maximum(jnp.maximum(m1, m2), m3)
            w1, w2, w3 = jnp.exp2(m1 - m), jnp.exp2(m2 - m), jnp.exp2(m3 - m)
            num = w1 * acc1 + w2 * part2_ref[0, s, qrows, :] + w3 * group3_rows(0, s, a0)
            den = w1 * l1 + w2 * part2_ref[2, s, qrows, :] + w3 * group3_rows(2, s, a0)
            o_ref[qrows, s * LANES:(s + 1) * LANES] = (num / den).astype(o_ref.dtype)
        return carry

    lax.fori_loop(0, seq // qb // DIL_GROUP, g1, 0)


def _dilated_attn(qkv, *, seq, mix_w):
    n3, tokens, _ = qkv.shape
    n_slab = n3 // 3
    return pl.pallas_call(
        functools.partial(_dilated_kernel, n_slab=n_slab),
        grid=(tokens // seq,),
        in_specs=[pl.BlockSpec((n3, seq, LANES), lambda i: (0, i, 0))],
        out_specs=pl.BlockSpec((seq, mix_w), lambda i: (i, 0)),
        out_shape=jax.ShapeDtypeStruct((tokens, mix_w), BF16),
        scratch_shapes=[pltpu.VMEM((3, n_slab, seq, LANES), F32),
                        pltpu.VMEM((3, n_slab, seq + seq // DIL_PAIRS[2][1], LANES), F32),
            pltpu.VMEM((3, HEADS_PER_SLAB * Q_BLOCK, 2 * Q_BLOCK), F32),
            pltpu.VMEM((1, HEADS_PER_SLAB * (seq // DIL_PAIRS[2][1]), seq // DIL_PAIRS[2][1]), F32)],
        compiler_params=_cparams("parallel"),
        name="dilated_attn",
    )(qkv)


def _swa_kernel(sink_ref, q_ref, k_ref, v_ref, o_ref, mask_ref, *, n_slab):
    seq = q_ref.shape[0]
    low = _low_head_mask()
    qb = Q_BLOCK
    nk = qb + 2 * SWA_WINDOW
    _fill_band_biases(mask_ref, qb, SWA_WINDOW)
    first_head = lax.broadcasted_iota(jnp.int32, (HEADS_PER_SLAB * qb, 1), 0) < qb

    def block_group(g, carry):
        chains = []
        for i in range(SWA_GROUP):
            a0 = pl.multiple_of((g * SWA_GROUP + i) * qb, qb)
            k0 = pl.multiple_of(jnp.clip(a0 - SWA_WINDOW, 0, seq - nk), qb)
            cls = (a0 - k0) // SWA_WINDOW
            for s in range(n_slab):
                chains.append((s, pl.ds(a0, qb), pl.ds(k0, nk), slice(s * LANES, (s + 1) * LANES), cls))
        def score(chain):
            s, qrows, krows, lanes, cls = chain
            return _qk(_stack_heads(q_ref[qrows, lanes], low), k_ref[krows, lanes]) + mask_ref[cls]

        def finish(chain, sc):
            s, qrows, krows, lanes, _ = chain
            sink = jnp.where(first_head, sink_ref[s * HEADS_PER_SLAB], sink_ref[s * HEADS_PER_SLAB + 1])
            m = jnp.maximum(jnp.max(sc, axis=-1, keepdims=True), sink)
            p = jnp.exp2(sc - m)
            denom = jnp.sum(p, axis=-1, keepdims=True) + jnp.exp2(sink - m)
            pv = jnp.dot(p.astype(BF16), v_ref[krows, lanes], preferred_element_type=F32)
            o_ref[qrows, lanes] = _unstack_heads(pv / denom, low).astype(o_ref.dtype)

        _pipeline_chains(chains, score, finish)
        return carry

    lax.fori_loop(0, seq // qb // SWA_GROUP, block_group, 0)


def _swa_attn(sink, q, k, v, *, seq):
    tokens, mix_w = q.shape
    seq_spec = pl.BlockSpec((seq, mix_w), lambda i: (i, 0))
    return pl.pallas_call(
        functools.partial(_swa_kernel, n_slab=mix_w // LANES),
        grid=(tokens // seq,),
        in_specs=[pl.BlockSpec(memory_space=pltpu.SMEM), seq_spec, seq_spec, seq_spec],
        out_specs=seq_spec,
        out_shape=jax.ShapeDtypeStruct((tokens, mix_w), BF16),
        scratch_shapes=[pltpu.VMEM((3, HEADS_PER_SLAB * Q_BLOCK, Q_BLOCK + 2 * SWA_WINDOW), F32)],
        compiler_params=_cparams("parallel"),
        name="swa_attn",
    )(sink, q, k, v)


def _conformer_rows(pad_ref, ya_ref, cw_ref, cb_ref, lg_ref, lb_ref, row_blocks):
    n_slab = pad_ref.shape[0]
    ch = n_slab * LANES
    half = (CONV_A_K - 1) // 2
    for c0 in row_blocks:
        accs = []
        for s in range(n_slab):
            lanes = slice(s * LANES, (s + 1) * LANES)
            acc = cb_ref[:, lanes]
            for k in range(CONV_A_K):
                r0 = c0 + CONV_PAD - half + k
                acc = acc + pad_ref[s, r0:r0 + CONV_CHUNK, :] * cw_ref[k:k + 1, lanes]
            accs.append(acc)
        mu = sum(jnp.sum(a, axis=-1, keepdims=True) for a in accs) * (1.0 / ch)
        ds = [a - mu for a in accs]
        var = sum(jnp.sum(d * d, axis=-1, keepdims=True) for d in ds) * (1.0 / ch)
        inv = lax.rsqrt(var + EPS)
        for s, d in enumerate(ds):
            lanes = slice(s * LANES, (s + 1) * LANES)
            y = d * inv * lg_ref[:, lanes] + lb_ref[:, lanes]
            ya_ref[c0:c0 + CONV_CHUNK, lanes] = (y * jax.nn.sigmoid(y)).astype(ya_ref.dtype)


def _merge_kernel(x_ref, g_ref, wg_ref, bg_ref, au_ref, aup_ref, aun_ref, cw_ref, cb_ref, lg_ref, lb_ref,
                  yb_ref, yc_ref, yd_ref, wb_ref, wo_ref, o_ref, pad_ref, ya_ref, *, seq_tiles):
    tm, d_model = x_ref.shape
    j = pl.program_id(0) % seq_tiles
    for s in range(pad_ref.shape[0]):
        lanes = slice(s * LANES, (s + 1) * LANES)
        pad_ref[s, 0:CONV_PAD, :] = jnp.where(j > 0, aup_ref[:, lanes], 0.0)
        pad_ref[s, CONV_PAD:CONV_PAD + tm, :] = au_ref[:, lanes]
        pad_ref[s, CONV_PAD + tm:2 * CONV_PAD + tm, :] = jnp.where(j < seq_tiles - 1, aun_ref[:, lanes], 0.0)

    x = x_ref[...]
    hb = _rmsnorm_rows(x, g_ref[...]).astype(BF16)

    def gate(n):
        cols = slice(n * d_model, (n + 1) * d_model)
        return jax.nn.sigmoid(jnp.dot(hb, wg_ref[:, cols], preferred_element_type=F32) + bg_ref[:, cols])

    def branch(n, y):
        return jnp.dot(y, wb_ref[n], preferred_element_type=F32)

    blocks = list(range(0, tm, CONV_CHUNK))
    others = [(1, yb_ref), (2, yc_ref), (3, yd_ref)]
    gates = {0: gate(0)}
    big = ([lambda n=n: gates.__setitem__(n, gate(n)) for n, _ in others]
           + [lambda n=n, y_ref=y_ref: gates[n] * branch(n, y_ref[...]) for n, y_ref in others])
    a_terms, other_terms = [], []
    for i in range(max(len(blocks), len(big))):
        if i < len(big):
            out = big[i]()
            if out is not None:
                other_terms.append(out)
        if i < len(blocks):
            c0 = blocks[i]
            _conformer_rows(pad_ref, ya_ref, cw_ref, cb_ref, lg_ref, lb_ref, [c0])
            a_terms.append(gates[0][c0:c0 + CONV_CHUNK] * branch(0, ya_ref[c0:c0 + CONV_CHUNK, :]))
    mixed = jnp.concatenate(a_terms, axis=0)
    for t in other_terms:
        mixed = mixed + t
    o_ref[...] = x + jnp.dot(mixed.astype(BF16), wo_ref[...], preferred_element_type=F32)


def _merge(x2, g, w_gate, b_gate, a_u, conv_w, conv_b, ln_g, ln_b, ys, w_branch, w_out, *, seq, tm):
    tokens, d_model = x2.shape
    mix_w = a_u.shape[1]
    seq_tiles = seq // tm
    halo_per_tile = tm // CONV_PAD
    n_halo = tokens // CONV_PAD
    tok_spec = lambda w: pl.BlockSpec((tm, w), lambda i: (i, 0))
    full = lambda shape: pl.BlockSpec(shape, lambda i: (0,) * len(shape))
    return pl.pallas_call(
        functools.partial(_merge_kernel, seq_tiles=seq_tiles),
        grid=(tokens // tm,),
        in_specs=[tok_spec(d_model), full(g.shape), full(w_gate.shape), full(b_gate.shape),
                  tok_spec(mix_w),
                  pl.BlockSpec((CONV_PAD, mix_w), lambda i: (jnp.maximum(i * halo_per_tile - 1, 0), 0)),
                  pl.BlockSpec((CONV_PAD, mix_w), lambda i: (jnp.minimum((i + 1) * halo_per_tile, n_halo - 1), 0)),
                  full(conv_w.shape), full(conv_b.shape), full(ln_g.shape), full(ln_b.shape)]
                 + [tok_spec(mix_w)] * len(ys)
                 + [full(w_branch.shape), full(w_out.shape)],
        out_specs=tok_spec(d_model),
        out_shape=jax.ShapeDtypeStruct((tokens, d_model), F32),
        scratch_shapes=[pltpu.VMEM((mix_w // LANES, tm + 2 * CONV_PAD, LANES), F32),
                        pltpu.VMEM((tm, mix_w), BF16)],
        compiler_params=_cparams("parallel"),
        name="merge",
    )(x2, g, w_gate, b_gate, a_u, a_u, a_u, conv_w, conv_b, ln_g, ln_b, *ys, w_branch, w_out)


FFN_CHUNK = 256


FFN_SLOTS = 2
FFN_TM = 512
FFN_DOWN_GROUP = 4


def _ffn_kernel(x_ref, xp_ref, xn_ref, g_ref, wu_ref, cw_ref, cb_ref, wd_ref, o_ref, u_ref, act_ref, *, d_ff):
    j = pl.program_id(1)
    tm = x_ref.shape[0]
    g = g_ref[...]
    hp = jnp.where(j > 0, _rmsnorm_rows(xp_ref[...], g), 0.0)
    hn = jnp.where(j < pl.num_programs(1) - 1, _rmsnorm_rows(xn_ref[...], g), 0.0)
    hcat = jnp.concatenate([hp, _rmsnorm_rows(x_ref[...], g), hn], axis=0).astype(BF16)

    n_chunk = d_ff // FFN_CHUNK
    n_slab = FFN_CHUNK // LANES

    def col0(c, half):
        return half * d_ff + c * FFN_CHUNK

    def up_project(c):
        for half in range(2):
            u = jnp.dot(hcat, wu_ref[:, col0(c, half):col0(c, half) + FFN_CHUNK], preferred_element_type=F32)
            for s in range(n_slab):
                u_ref[c % FFN_SLOTS, half * n_slab + s] = u[:, s * LANES:(s + 1) * LANES]

    def conv_act(c):
        def conv(half, s):
            lanes = slice(col0(c, half) + s * LANES, col0(c, half) + (s + 1) * LANES)
            y = cb_ref[:, lanes]
            for k in range(FFN_CONV_K):
                rows = slice(SUBLANES - 1 + k, SUBLANES - 1 + k + tm)
                y = y + u_ref[c % FFN_SLOTS, half * n_slab + s, rows, :] * cw_ref[k:k + 1, lanes]
            return y

        for s in range(n_slab):
            gate, up = conv(0, s), conv(1, s)
            out_lanes = slice(c * FFN_CHUNK + s * LANES, c * FFN_CHUNK + (s + 1) * LANES)
            act_ref[:, out_lanes] = (gate * jax.nn.sigmoid(gate) * up).astype(BF16)

    out = x_ref[...]
    group_start = 0
    up_project(0)
    for c in range(n_chunk):
        if c + 1 < n_chunk:
            up_project(c + 1)
        conv_act(c)
        if (c + 1) % FFN_DOWN_GROUP == 0 or c + 1 == n_chunk:
            rows = slice(group_start * FFN_CHUNK, (c + 1) * FFN_CHUNK)
            out = out + jnp.dot(act_ref[:, rows], wd_ref[rows, :], preferred_element_type=F32)
            group_start = c + 1
    o_ref[...] = out


def _ffn(x2, g, w_up, conv_w, conv_b, w_down, *, seq, tm):
    tokens, d_model = x2.shape
    d_ff = w_down.shape[0]
    seq_tiles = seq // tm
    halo_per_tile = tm // SUBLANES
    n_halo = tokens // SUBLANES
    full = lambda shape: pl.BlockSpec(shape, lambda b, j: (0,) * len(shape), pipeline_mode=pl.Buffered(1))
    return pl.pallas_call(
        functools.partial(_ffn_kernel, d_ff=d_ff),
        grid=(tokens // seq, seq_tiles),
        in_specs=[
            pl.BlockSpec((tm, d_model), lambda b, j: (b * seq_tiles + j, 0)),
            pl.BlockSpec((SUBLANES, d_model),
                         lambda b, j: (jnp.maximum((b * seq_tiles + j) * halo_per_tile - 1, 0), 0)),
            pl.BlockSpec((SUBLANES, d_model),
                         lambda b, j: (jnp.minimum((b * seq_tiles + j + 1) * halo_per_tile, n_halo - 1), 0)),
            full(g.shape), full(w_up.shape), full(conv_w.shape), full(conv_b.shape), full(w_down.shape),
        ],
        out_specs=pl.BlockSpec((tm, d_model), lambda b, j: (b * seq_tiles + j, 0)),
        out_shape=jax.ShapeDtypeStruct((tokens, d_model), F32),
        scratch_shapes=[pltpu.VMEM((FFN_SLOTS, 2 * FFN_CHUNK // LANES, tm + 2 * SUBLANES, LANES), F32),
                        pltpu.VMEM((tm, d_ff), BF16)],
        compiler_params=_cparams("parallel", "parallel"),
        name="conv_ffn",
    )(x2, x2, x2, g, w_up, conv_w, conv_b, w_down)


def _rope_lane_tables(seq):
    half = ROPE_DIMS // 2
    pos = jnp.arange(seq, dtype=F32)
    inv = ROPE_THETA ** (-jnp.arange(0, ROPE_DIMS, 2, dtype=F32) / ROPE_DIMS)
    ang = pos[:, None] * inv[None, :]
    cos, sin = jnp.cos(ang), jnp.sin(ang)
    d = np.arange(LANES) % HEAD_DIM
    first, second = d < half, (d >= half) & (d < ROPE_DIMS)
    idx = d % half
    c_tab = jnp.where((first | second)[None, :], cos[:, idx], 1.0)
    a_tab = jnp.where(first[None, :], -sin[:, idx], 0.0)
    b_tab = jnp.where(second[None, :], sin[:, idx], 0.0)
    return jnp.stack([c_tab, a_tab, b_tab]).astype(F32)


def kernel(x, g_mix, w_in, gate_b, a_conv_w, a_conv_b, a_ln_g, a_ln_b, na_qn, na_kn, na_rpb, dil_qn, dil_kn, swa_qn, swa_kn, swa_sink, w_branch, w_out, g_ffn, w_up, ffn_conv_w, ffn_conv_b, w_down):
    bsz, seq, d_model = x.shape
    depth = w_in.shape[0]
    mix_w = d_model // N_BRANCH
    heads = mix_w // HEAD_DIM
    kv_heads = swa_sink.shape[1] // 2
    grp = heads // kv_heads
    n_gate = N_BRANCH * d_model
    assert mix_w % LANES == 0 and seq % (GRID_W * NA_ROWS) == 0
    assert all(w // (2 * d) == DIL_PAIRS[0][0] // 2 for w, d in DIL_PAIRS)

    rope_tab = _rope_lane_tables(seq)
    scale = HEAD_DIM ** -0.5 * LOG2_E
    tile_h = lambda v: jnp.tile(v.astype(F32), heads)
    x2 = x.reshape(bsz * seq, d_model)
    na_bias = _na_bias_table(na_rpb.astype(F32) * LOG2_E, seq // GRID_W)

    for l in range(depth):
        w_l = w_in[l]
        o_swk = w_l.shape[1] - 2 * kv_heads * HEAD_DIM
        rep_heads = lambda w: jnp.repeat(w.reshape(d_model, kv_heads, HEAD_DIM), grp, axis=1).reshape(d_model, mix_w)
        w_br = jnp.concatenate([
            w_l[:, n_gate:o_swk],
            rep_heads(w_l[:, o_swk:o_swk + kv_heads * HEAD_DIM]),
            rep_heads(w_l[:, o_swk + kv_heads * HEAD_DIM:]),
        ], axis=1).astype(BF16)
        gains = jnp.stack([tile_h(na_qn[l]) * scale, tile_h(na_kn[l]),
                           tile_h(dil_qn[l]) * scale, tile_h(dil_kn[l]),
                           tile_h(swa_qn[l]) * scale, tile_h(swa_kn[l]),
                           jnp.zeros((mix_w,), F32), jnp.zeros((mix_w,), F32)])

        a_u, na_q, na_k, na_v, dil_qkv, sw_q, sw_k, sw_v = _inproj(
            x2, g_mix[l][None, :], w_br, gains, rope_tab, seq=seq, tm=512)
        y_b = _na_attn(na_q, na_k, na_v, na_bias[l], seq=seq)
        y_c = _dilated_attn(dil_qkv, seq=seq, mix_w=mix_w)
        y_d = _swa_attn(swa_sink[l].astype(F32) * LOG2_E, sw_q, sw_k, sw_v, seq=seq)
        x2 = _merge(x2, g_mix[l][None, :], w_l[:, :n_gate].astype(BF16), gate_b[l].reshape(1, n_gate),
                    a_u, a_conv_w[l], a_conv_b[l][None, :], a_ln_g[l][None, :], a_ln_b[l][None, :],
                    (y_b, y_c, y_d), w_branch[l].astype(BF16), w_out[l].astype(BF16), seq=seq, tm=512)
        x2 = _ffn(x2, g_ffn[l][None, :], w_up[l].astype(BF16), ffn_conv_w[l], ffn_conv_b[l][None, :],
                  w_down[l].astype(BF16), seq=seq, tm=FFN_TM)
    return x2.reshape(bsz, seq, d_model)
```

```python
import functools

import numpy as np
import jax
import jax.numpy as jnp
from jax import lax
from jax.experimental import pallas as pl
from jax.experimental.pallas import tpu as pltpu

F32 = jnp.float32
BF16 = jnp.bfloat16

N_BRANCH = 4
HEAD_DIM = 64
CONV_A_K = 31
NA_ROWS = 8
NA_COLS = 16
GRID_W = 64
DIL_PAIRS = ((128, 1), (512, 4), (2048, 16))
SWA_WINDOW = 128
Q_BLOCK = 128
ROPE_THETA = 500000.0
ROPE_DIMS = HEAD_DIM // 4
FFN_CONV_K = 3
EPS = 1e-6
NEG_INF = -1e30
LOG2_E = 1.4426950408889634

LANES = 128
SUBLANES = 8
VMEM_LIMIT = 56 * 1024 * 1024

HEADS_PER_SLAB = LANES // HEAD_DIM


def _cparams(*sem):
    return pltpu.CompilerParams(dimension_semantics=sem, vmem_limit_bytes=VMEM_LIMIT)


def _low_head_mask():
    return lax.broadcasted_iota(jnp.int32, (1, LANES), 1) < HEAD_DIM


def _rmsnorm_rows(x, g):
    return x * lax.rsqrt(jnp.mean(x * x, axis=-1, keepdims=True) + EPS) * g


def _rope(y, rope_ref):
    half = ROPE_DIMS // 2
    return (y * rope_ref[0]
            + pltpu.roll(y, LANES - half, 1) * rope_ref[1]
            + pltpu.roll(y, half, 1) * rope_ref[2])


(SEG_GLU_A, SEG_GLU_G, SEG_NA_Q, SEG_NA_K, SEG_NA_V, SEG_DIL_Q, SEG_DIL_K, SEG_DIL_V,
 SEG_SW_Q, SEG_SW_K, SEG_SW_V) = range(11)
NORMED_SEGS = (SEG_NA_Q, SEG_NA_K, SEG_DIL_Q, SEG_DIL_K, SEG_SW_Q, SEG_SW_K)


def _inproj_kernel(x_ref, g_ref, w_ref, gains_ref, rope_ref,
                   a_ref, naq_ref, nak_ref, nav_ref, dil_ref, swq_ref, swk_ref, swv_ref, *, mix_w):
    n_slab = mix_w // LANES
    hb = _rmsnorm_rows(x_ref[...], g_ref[...]).astype(BF16)
    r = lax.broadcasted_iota(jnp.int32, (mix_w, mix_w), 0) // HEAD_DIM
    c = lax.broadcasted_iota(jnp.int32, (mix_w, mix_w), 1) // HEAD_DIM
    blockdiag = (r == c).astype(BF16)

    def proj(seg):
        return jnp.dot(hb, w_ref[:, seg * mix_w:(seg + 1) * mix_w], preferred_element_type=F32)

    def head_norm(row, v):
        ms = jnp.dot((v * v).astype(BF16), blockdiag, preferred_element_type=F32) * (1.0 / HEAD_DIM)
        return v * lax.rsqrt(ms + EPS) * gains_ref[row:row + 1, :]

    p, normed = {}, {}
    for row, seg in enumerate(NORMED_SEGS):
        p[seg] = proj(seg)
        if row:
            normed[NORMED_SEGS[row - 1]] = head_norm(row - 1, p[NORMED_SEGS[row - 1]])
    for seg in range(SEG_SW_V + 1):
        if seg not in p:
            p[seg] = proj(seg)
            if NORMED_SEGS[-1] not in normed:
                normed[NORMED_SEGS[-1]] = head_norm(len(NORMED_SEGS) - 1, p[NORMED_SEGS[-1]])

    def slabs(v):
        return [v[:, s * LANES:(s + 1) * LANES] for s in range(n_slab)]

    a_ref[...] = p[SEG_GLU_A] * jax.nn.sigmoid(p[SEG_GLU_G])
    naq_ref[...] = normed[SEG_NA_Q].astype(BF16)
    nak_ref[...] = normed[SEG_NA_K].astype(BF16)
    nav_ref[...] = p[SEG_NA_V].astype(BF16)
    for s in range(n_slab):
        dil_ref[s] = _rope(slabs(normed[SEG_DIL_Q])[s], rope_ref)
        dil_ref[n_slab + s] = _rope(slabs(normed[SEG_DIL_K])[s], rope_ref)
        dil_ref[2 * n_slab + s] = slabs(p[SEG_DIL_V])[s]
        lanes = slice(s * LANES, (s + 1) * LANES)
        swq_ref[:, lanes] = _rope(slabs(normed[SEG_SW_Q])[s], rope_ref).astype(BF16)
        swk_ref[:, lanes] = _rope(slabs(normed[SEG_SW_K])[s], rope_ref).astype(BF16)
    swv_ref[...] = p[SEG_SW_V].astype(BF16)


def _inproj(x2, g, w_br, gains, rope_tab, *, seq, tm):
    tokens, d_model = x2.shape
    n_br = w_br.shape[1]
    mix_w = gains.shape[1]
    n_slab = mix_w // LANES
    seq_tiles = seq // tm
    tok_spec = lambda w: pl.BlockSpec((tm, w), lambda i: (i, 0))
    full = lambda shape: pl.BlockSpec(shape, lambda i: (0,) * len(shape))
    out_shape = (
        jax.ShapeDtypeStruct((tokens, mix_w), F32),
        jax.ShapeDtypeStruct((tokens, mix_w), BF16),
        jax.ShapeDtypeStruct((tokens, mix_w), BF16),
        jax.ShapeDtypeStruct((tokens, mix_w), BF16),
        jax.ShapeDtypeStruct((3 * n_slab, tokens, LANES), F32),
        jax.ShapeDtypeStruct((tokens, mix_w), BF16),
        jax.ShapeDtypeStruct((tokens, mix_w), BF16),
        jax.ShapeDtypeStruct((tokens, mix_w), BF16),
    )
    out_specs = (
        tok_spec(mix_w), tok_spec(mix_w), tok_spec(mix_w), tok_spec(mix_w),
        pl.BlockSpec((3 * n_slab, tm, LANES), lambda i: (0, i, 0)),
        tok_spec(mix_w), tok_spec(mix_w), tok_spec(mix_w),
    )
    return pl.pallas_call(
        functools.partial(_inproj_kernel, mix_w=mix_w),
        grid=(tokens // tm,),
        in_specs=[
            tok_spec(d_model),
            full((1, d_model)),
            full((d_model, n_br)),
            full(gains.shape),
            pl.BlockSpec((3, tm, LANES), lambda i: (0, i % seq_tiles, 0)),
        ],
        out_specs=out_specs,
        out_shape=out_shape,
        compiler_params=_cparams("parallel"),
        name="inproj",
    )(x2, g, w_br, gains, rope_tab)


CONV_CHUNK = 128
CONV_PAD = 16


NA_GROUP = 16
DIL_GROUP = 16
SWA_GROUP = 16


def _stack_heads(q_slab, low):
    zero = jnp.zeros_like(q_slab)
    return jnp.concatenate([jnp.where(low, q_slab, zero), jnp.where(low, zero, q_slab)], axis=0)


def _unstack_heads(o, low):
    m = o.shape[0] // HEADS_PER_SLAB
    return jnp.where(low, o[:m], o[m:])


def _qk(q_stack, k_slab):
    return lax.dot_general(q_stack, k_slab, (((1,), (1,)), ((), ())), preferred_element_type=F32)


ATTN_AHEAD = 2
NA_AHEAD = 4


def _pipeline_chains(chains, score, finish, ahead=ATTN_AHEAD):
    pending, outs = [], []
    for chain in chains:
        pending.append((chain, score(chain)))
        if len(pending) > ahead:
            outs.append(finish(*pending.pop(0)))
    while pending:
        outs.append(finish(*pending.pop(0)))
    return outs


def _band_bias(nq, nk, shift, side):
    qi = lax.broadcasted_iota(jnp.int32, (HEADS_PER_SLAB * nq, nk), 0) % nq
    ki = lax.broadcasted_iota(jnp.int32, (HEADS_PER_SLAB * nq, nk), 1)
    return jnp.where(jnp.abs(qi + shift - ki) <= side, 0.0, NEG_INF).astype(F32)


def _fill_band_biases(mask_ref, nq, side):
    for c in range(mask_ref.shape[0]):
        mask_ref[c] = _band_bias(nq, mask_ref.shape[2], c * side, side)


def _na_kernel(q_ref, k_ref, v_ref, bias_ref, o_ref, *, rows, n_slab):
    low = _low_head_mask()
    kr = NA_ROWS
    n_edge_lo = kr // 2 + 1
    mid_hi = rows - (kr - kr // 2)

    def row_group(g, carry):
        chains = []
        for i in range(NA_GROUP):
            r = g * NA_GROUP + i
            row_start = jnp.clip(r - kr // 2, 0, rows - kr)
            cls = jnp.where(r < n_edge_lo, r, jnp.where(r <= mid_hi, n_edge_lo, r - mid_hi + n_edge_lo))
            q0 = pl.multiple_of(r * GRID_W, GRID_W)
            k0 = pl.multiple_of(row_start * GRID_W, GRID_W)
            for s in range(n_slab):
                chains.append((cls, s, pl.ds(q0, GRID_W), pl.ds(k0, kr * GRID_W),
                               slice(s * LANES, (s + 1) * LANES)))
        def score(chain):
            cls, s, qrows, krows, lanes = chain
            return _qk(_stack_heads(q_ref[qrows, lanes], low), k_ref[krows, lanes]) + bias_ref[cls, s]

        def finish(chain, sc):
            _, _, qrows, krows, lanes = chain
            p = jnp.exp2(sc - jnp.max(sc, axis=-1, keepdims=True))
            l = jnp.sum(p, axis=-1, keepdims=True)
            pv = jnp.dot(p.astype(BF16), v_ref[krows, lanes], preferred_element_type=F32)
            o_ref[qrows, lanes] = _unstack_heads(pv / l, low).astype(o_ref.dtype)

        _pipeline_chains(chains, score, finish, ahead=NA_AHEAD)
        return carry

    lax.fori_loop(0, rows // NA_GROUP, row_group, 0)


def _na_bias_table(rpb, rows):
    kr, kc = NA_ROWS, NA_COLS
    depth, heads = rpb.shape[:2]
    n_edge_lo = kr // 2 + 1
    mid_hi = rows - (kr - kr // 2)
    rep = np.asarray(list(range(n_edge_lo)) + [n_edge_lo] + list(range(mid_hi + 1, rows)))
    row_start = np.clip(rep - kr // 2, 0, rows - kr)
    dr = row_start[:, None] + np.arange(kr)[None, :] - rep[:, None] + (NA_ROWS - 1)
    col = np.arange(GRID_W)
    col_start = np.clip(col - kc // 2, 0, GRID_W - kc)
    valid = (col[None, :] >= col_start[:, None]) & (col[None, :] < col_start[:, None] + kc)
    dc = col[None, :] - col[:, None] + (NA_COLS - 1)
    sel_r = (dr[:, :, None] == np.arange(2 * NA_ROWS - 1)).astype(np.float32)
    sel_c = ((dc[:, :, None] == np.arange(2 * NA_COLS - 1)) & valid[:, :, None]).astype(np.float32)
    tab = jnp.einsum('kir,lhrj,wcj->lkhwic', sel_r, rpb.astype(F32), sel_c,
                     precision=lax.Precision.HIGHEST)
    tab = tab + jnp.where(valid, 0.0, NEG_INF).astype(F32)[None, None, None, :, None, :]
    return tab.reshape(depth, len(rep), heads // HEADS_PER_SLAB, HEADS_PER_SLAB * GRID_W, kr * GRID_W)


def _na_attn(q, k, v, bias, *, seq):
    tokens, mix_w = q.shape
    seq_spec = pl.BlockSpec((seq, mix_w), lambda i: (i, 0))
    return pl.pallas_call(
        functools.partial(_na_kernel, rows=seq // GRID_W, n_slab=mix_w // LANES),
        grid=(tokens // seq,),
        in_specs=[seq_spec, seq_spec, seq_spec,
                  pl.BlockSpec(bias.shape, lambda i: (0, 0, 0, 0))],
        out_specs=seq_spec,
        out_shape=jax.ShapeDtypeStruct((tokens, mix_w), BF16),
        compiler_params=_cparams("parallel"),
        name="na_attn",
    )(q, k, v, bias)


def _dil_group_blocks(chains, low):
    def score(chain):
        lq, lk, _, lb = chain
        return _qk(_stack_heads(lq().astype(BF16), low), lk().astype(BF16)) + lb()

    def finish(chain, sc):
        m = jnp.max(sc, axis=-1, keepdims=True)
        p = jnp.exp2(sc - m)
        l = jnp.sum(p, axis=-1, keepdims=True)
        acc = jnp.dot(p.astype(BF16), chain[2]().astype(BF16), preferred_element_type=F32)
        nq = acc.shape[0] // HEADS_PER_SLAB
        return (_unstack_heads(acc, low), jnp.where(low, m[:nq], m[nq:]), jnp.where(low, l[:nq], l[nq:]))

    return _pipeline_chains(chains, score, finish)


def _dilated_kernel(qkv_ref, o_ref, part2_ref, part3_ref, mask_ref, mask_sq_ref, *, n_slab):
    seq = qkv_ref.shape[1]
    low = _low_head_mask()
    (_, dil1), (_, dil2), (_, dil3) = DIL_PAIRS
    side = DIL_PAIRS[0][0] // (2 * dil1)
    qb = Q_BLOCK
    len2, len3 = seq // dil2, seq // dil3
    nblk2 = len2 // qb
    pitch3 = dil3 + 1
    _fill_band_biases(mask_ref, qb, side)
    _fill_band_biases(mask_sq_ref, len3, side)

    def chain(s, qrows, krows, bias):
        load = lambda kind, rows: (lambda: qkv_ref[kind * n_slab + s, rows, :])
        return (load(0, qrows), load(1, krows), load(2, krows), bias)

    def wide_bias(a0, k0):
        return lambda: mask_ref[(a0 - k0) // side]

    def g3(g, carry):
        blocks = [(s, pl.ds(g * DIL_GROUP + i, len3, stride=dil3))
                  for i in range(DIL_GROUP) for s in range(n_slab)]
        outs = _dil_group_blocks([chain(s, rows, rows, lambda: mask_sq_ref[0]) for s, rows in blocks], low)
        for i, parts in enumerate(outs):
            s = i % n_slab
            rows = pl.ds(g * DIL_GROUP + i // n_slab, len3, stride=pitch3)
            for kind, part in enumerate(parts):
                part3_ref[kind, s, rows, :] = part
        return carry

    lax.fori_loop(0, dil3 // DIL_GROUP, g3, 0)

    def g2(g, carry):
        blocks, chains = [], []
        for i in range(DIL_GROUP):
            t = g * DIL_GROUP + i
            b = t // nblk2
            a0 = (t % nblk2) * qb
            k0 = jnp.clip(a0 - side, 0, len2 - 2 * qb)
            qrows = pl.ds(b + dil2 * a0, qb, stride=dil2)
            krows = pl.ds(b + dil2 * k0, 2 * qb, stride=dil2)
            for s in range(n_slab):
                blocks.append((s, qrows))
                chains.append(chain(s, qrows, krows, wide_bias(a0, k0)))
        for (s, qrows), parts in zip(blocks, _dil_group_blocks(chains, low)):
            for kind, part in enumerate(parts):
                part2_ref[kind, s, qrows, :] = part
        return carry

    lax.fori_loop(0, dil2 * nblk2 // DIL_GROUP, g2, 0)

    def group3_rows(kind, s, a0):
        base = a0 + a0 // dil3
        return jnp.concatenate([part3_ref[kind, s, pl.ds(base + pitch3 * i, dil3, stride=1), :]
                                for i in range(qb // dil3)], axis=0)

    def g1(g, carry):
        blocks, chains = [], []
        for i in range(DIL_GROUP):
            a0 = pl.multiple_of((g * DIL_GROUP + i) * qb, qb)
            k0 = pl.multiple_of(jnp.clip(a0 - side, 0, seq - 2 * qb), side)
            for s in range(n_slab):
                blocks.append((s, a0))
                chains.append(chain(s, pl.ds(a0, qb), pl.ds(k0, 2 * qb), wide_bias(a0, k0)))
        for (s, a0), (acc1, m1, l1) in zip(blocks, _dil_group_blocks(chains, low)):
            qrows = pl.ds(a0, qb)
            m2, m3 = part2_ref[1, s, qrows, :], group3_rows(1, s, a0)
            m = jnp.maximum(jnp.maximum(m1, m2), m3)
            w1, w2, w3 = jnp.exp2(m1 - m), jnp.exp2(m2 - m), jnp.exp2(m3 - m)
            num = w1 * acc1 + w2 * part2_ref[0, s, qrows, :] + w3 * group3_rows(0, s, a0)
            den = w1 * l1 + w2 * part2_ref[2, s, qrows, :] + w3 * group3_rows(2, s, a0)
            o_ref[qrows, s * LANES:(s + 1) * LANES] = (num / den).astype(o_ref.dtype)
        return carry

    lax.fori_loop(0, seq // qb // DIL_GROUP, g1, 0)


def _dilated_attn(qkv, *, seq, mix_w):
    n3, tokens, _ = qkv.shape
    n_slab = n3 // 3
    return pl.pallas_call(
        functools.partial(_dilated_kernel, n_slab=n_slab),
        grid=(tokens // seq,),
        in_specs=[pl.BlockSpec((n3, seq, LANES), lambda i: (0, i, 0))],
        out_specs=pl.BlockSpec((seq, mix_w), lambda i: (i, 0)),
        out_shape=jax.ShapeDtypeStruct((tokens, mix_w), BF16),
        scratch_shapes=[pltpu.VMEM((3, n_slab, seq, LANES), F32),
                        pltpu.VMEM((3, n_slab, seq + seq // DIL_PAIRS[2][1], LANES), F32),
            pltpu.VMEM((3, HEADS_PER_SLAB * Q_BLOCK, 2 * Q_BLOCK), F32),
            pltpu.VMEM((1, HEADS_PER_SLAB * (seq // DIL_PAIRS[2][1]), seq // DIL_PAIRS[2][1]), F32)],
        compiler_params=_cparams("parallel"),
        name="dilated_attn",
    )(qkv)


def _swa_kernel(sink_ref, q_ref, k_ref, v_ref, o_ref, mask_ref, *, n_slab):
    seq = q_ref.shape[0]
    low = _low_head_mask()
    qb = Q_BLOCK
    nk = qb + 2 * SWA_WINDOW
    _fill_band_biases(mask_ref, qb, SWA_WINDOW)
    first_head = lax.broadcasted_iota(jnp.int32, (HEADS_PER_SLAB * qb, 1), 0) < qb

    def block_group(g, carry):
        chains = []
        for i in range(SWA_GROUP):
            a0 = pl.multiple_of((g * SWA_GROUP + i) * qb, qb)
            k0 = pl.multiple_of(jnp.clip(a0 - SWA_WINDOW, 0, seq - nk), qb)
            cls = (a0 - k0) // SWA_WINDOW
            for s in range(n_slab):
                chains.append((s, pl.ds(a0, qb), pl.ds(k0, nk), slice(s * LANES, (s + 1) * LANES), cls))
        def score(chain):
            s, qrows, krows, lanes, cls = chain
            return _qk(_stack_heads(q_ref[qrows, lanes], low), k_ref[krows, lanes]) + mask_ref[cls]

        def finish(chain, sc):
            s, qrows, krows, lanes, _ = chain
            sink = jnp.where(first_head, sink_ref[s * HEADS_PER_SLAB], sink_ref[s * HEADS_PER_SLAB + 1])
            m = jnp.maximum(jnp.max(sc, axis=-1, keepdims=True), sink)
            p = jnp.exp2(sc - m)
            denom = jnp.sum(p, axis=-1, keepdims=True) + jnp.exp2(sink - m)
            pv = jnp.dot(p.astype(BF16), v_ref[krows, lanes], preferred_element_type=F32)
            o_ref[qrows, lanes] = _unstack_heads(pv / denom, low).astype(o_ref.dtype)

        _pipeline_chains(chains, score, finish)
        return carry

    lax.fori_loop(0, seq // qb // SWA_GROUP, block_group, 0)


def _swa_attn(sink, q, k, v, *, seq):
    tokens, mix_w = q.shape
    seq_spec = pl.BlockSpec((seq, mix_w), lambda i: (i, 0))
    return pl.pallas_call(
        functools.partial(_swa_kernel, n_slab=mix_w // LANES),
        grid=(tokens // seq,),
        in_specs=[pl.BlockSpec(memory_space=pltpu.SMEM), seq_spec, seq_spec, seq_spec],
        out_specs=seq_spec,
        out_shape=jax.ShapeDtypeStruct((tokens, mix_w), BF16),
        scratch_shapes=[pltpu.VMEM((3, HEADS_PER_SLAB * Q_BLOCK, Q_BLOCK + 2 * SWA_WINDOW), F32)],
        compiler_params=_cparams("parallel"),
        name="swa_attn",
    )(sink, q, k, v)


def _conformer_rows(pad_ref, ya_ref, cw_ref, cb_ref, lg_ref, lb_ref, row_blocks):
    n_slab = pad_ref.shape[0]
    ch = n_slab * LANES
    half = (CONV_A_K - 1) // 2
    for c0 in row_blocks:
        accs = []
        for s in range(n_slab):
            lanes = slice(s * LANES, (s + 1) * LANES)
            acc = cb_ref[:, lanes]
            for k in range(CONV_A_K):
                r0 = c0 + CONV_PAD - half + k
                acc = acc + pad_ref[s, r0:r0 + CONV_CHUNK, :] * cw_ref[k:k + 1, lanes]
            accs.append(acc)
        mu = sum(jnp.sum(a, axis=-1, keepdims=True) for a in accs) * (1.0 / ch)
        ds = [a - mu for a in accs]
        var = sum(jnp.sum(d * d, axis=-1, keepdims=True) for d in ds) * (1.0 / ch)
        inv = lax.rsqrt(var + EPS)
        for s, d in enumerate(ds):
            lanes = slice(s * LANES, (s + 1) * LANES)
            y = d * inv * lg_ref[:, lanes] + lb_ref[:, lanes]
            ya_ref[c0:c0 + CONV_CHUNK, lanes] = (y * jax.nn.sigmoid(y)).astype(ya_ref.dtype)


def _merge_kernel(x_ref, g_ref, wg_ref, bg_ref, au_ref, aup_ref, aun_ref, cw_ref, cb_ref, lg_ref, lb_ref,
                  yb_ref, yc_ref, yd_ref, wb_ref, wo_ref, o_ref, pad_ref, ya_ref, *, seq_tiles):
    tm, d_model = x_ref.shape
    j = pl.program_id(0) % seq_tiles
    for s in range(pad_ref.shape[0]):
        lanes = slice(s * LANES, (s + 1) * LANES)
        pad_ref[s, 0:CONV_PAD, :] = jnp.where(j > 0, aup_ref[:, lanes], 0.0)
        pad_ref[s, CONV_PAD:CONV_PAD + tm, :] = au_ref[:, lanes]
        pad_ref[s, CONV_PAD + tm:2 * CONV_PAD + tm, :] = jnp.where(j < seq_tiles - 1, aun_ref[:, lanes], 0.0)

    x = x_ref[...]
    hb = _rmsnorm_rows(x, g_ref[...]).astype(BF16)

    def gate(n):
        cols = slice(n * d_model, (n + 1) * d_model)
        return jax.nn.sigmoid(jnp.dot(hb, wg_ref[:, cols], preferred_element_type=F32) + bg_ref[:, cols])

    def branch(n, y):
        return jnp.dot(y, wb_ref[n], preferred_element_type=F32)

    blocks = list(range(0, tm, CONV_CHUNK))
    others = [(1, yb_ref), (2, yc_ref), (3, yd_ref)]
    gates = {0: gate(0)}
    big = ([lambda n=n: gates.__setitem__(n, gate(n)) for n, _ in others]
           + [lambda n=n, y_ref=y_ref: gates[n] * branch(n, y_ref[...]) for n, y_ref in others])
    a_terms, other_terms = [], []
    for i in range(max(len(blocks), len(big))):
        if i < len(big):
            out = big[i]()
            if out is not None:
                other_terms.append(out)
        if i < len(blocks):
            c0 = blocks[i]
            _conformer_rows(pad_ref, ya_ref, cw_ref, cb_ref, lg_ref, lb_ref, [c0])
            a_terms.append(gates[0][c0:c0 + CONV_CHUNK] * branch(0, ya_ref[c0:c0 + CONV_CHUNK, :]))
    mixed = jnp.concatenate(a_terms, axis=0)
    for t in other_terms:
        mixed = mixed + t
    o_ref[...] = x + jnp.dot(mixed.astype(BF16), wo_ref[...], preferred_element_type=F32)


def _merge(x2, g, w_gate, b_gate, a_u, conv_w, conv_b, ln_g, ln_b, ys, w_branch, w_out, *, seq, tm):
    tokens, d_model = x2.shape
    mix_w = a_u.shape[1]
    seq_tiles = seq // tm
    halo_per_tile = tm // CONV_PAD
    n_halo = tokens // CONV_PAD
    tok_spec = lambda w: pl.BlockSpec((tm, w), lambda i: (i, 0))
    full = lambda shape: pl.BlockSpec(shape, lambda i: (0,) * len(shape))
    return pl.pallas_call(
        functools.partial(_merge_kernel, seq_tiles=seq_tiles),
        grid=(tokens // tm,),
        in_specs=[tok_spec(d_model), full(g.shape), full(w_gate.shape), full(b_gate.shape),
                  tok_spec(mix_w),
                  pl.BlockSpec((CONV_PAD, mix_w), lambda i: (jnp.maximum(i * halo_per_tile - 1, 0), 0)),
                  pl.BlockSpec((CONV_PAD, mix_w), lambda i: (jnp.minimum((i + 1) * halo_per_tile, n_halo - 1), 0)),
                  full(conv_w.shape), full(conv_b.shape), full(ln_g.shape), full(ln_b.shape)]
                 + [tok_spec(mix_w)] * len(ys)
                 + [full(w_branch.shape), full(w_out.shape)],
        out_specs=tok_spec(d_model),
        out_shape=jax.ShapeDtypeStruct((tokens, d_model), F32),
        scratch_shapes=[pltpu.VMEM((mix_w // LANES, tm + 2 * CONV_PAD, LANES), F32),
                        pltpu.VMEM((tm, mix_w), BF16)],
        compiler_params=_cparams("parallel"),
        name="merge",
    )(x2, g, w_gate, b_gate, a_u, a_u, a_u, conv_w, conv_b, ln_g, ln_b, *ys, w_branch, w_out)


FFN_CHUNK = 256


FFN_SLOTS = 2
FFN_TM = 512
FFN_DOWN_GROUP = 4


def _ffn_kernel(x_ref, xp_ref, xn_ref, g_ref, wu_ref, cw_ref, cb_ref, wd_ref, o_ref, u_ref, act_ref, *, d_ff):
    j = pl.program_id(1)
    tm = x_ref.shape[0]
    g = g_ref[...]
    hp = jnp.where(j > 0, _rmsnorm_rows(xp_ref[...], g), 0.0)
    hn = jnp.where(j < pl.num_programs(1) - 1, _rmsnorm_rows(xn_ref[...], g), 0.0)
    hcat = jnp.concatenate([hp, _rmsnorm_rows(x_ref[...], g), hn], axis=0).astype(BF16)

    n_chunk = d_ff // FFN_CHUNK
    n_slab = FFN_CHUNK // LANES

    def col0(c, half):
        return half * d_ff + c * FFN_CHUNK

    def up_project(c):
        for half in range(2):
            u = jnp.dot(hcat, wu_ref[:, col0(c, half):col0(c, half) + FFN_CHUNK], preferred_element_type=F32)
            for s in range(n_slab):
                u_ref[c % FFN_SLOTS, half * n_slab + s] = u[:, s * LANES:(s + 1) * LANES]

    def conv_act(c):
        def conv(half, s):
            lanes = slice(col0(c, half) + s * LANES, col0(c, half) + (s + 1) * LANES)
            y = cb_ref[:, lanes]
            for k in range(FFN_CONV_K):
                rows = slice(SUBLANES - 1 + k, SUBLANES - 1 + k + tm)
                y = y + u_ref[c % FFN_SLOTS, half * n_slab + s, rows, :] * cw_ref[k:k + 1, lanes]
            return y

        for s in range(n_slab):
            gate, up = conv(0, s), conv(1, s)
            out_lanes = slice(c * FFN_CHUNK + s * LANES, c * FFN_CHUNK + (s + 1) * LANES)
            act_ref[:, out_lanes] = (gate * jax.nn.sigmoid(gate) * up).astype(BF16)

    out = x_ref[...]
    group_start = 0
    up_project(0)
    for c in range(n_chunk):
        if c + 1 < n_chunk:
            up_project(c + 1)
        conv_act(c)
        if (c + 1) % FFN_DOWN_GROUP == 0 or c + 1 == n_chunk:
            rows = slice(group_start * FFN_CHUNK, (c + 1) * FFN_CHUNK)
            out = out + jnp.dot(act_ref[:, rows], wd_ref[rows, :], preferred_element_type=F32)
            group_start = c + 1
    o_ref[...] = out


def _ffn(x2, g, w_up, conv_w, conv_b, w_down, *, seq, tm):
    tokens, d_model = x2.shape
    d_ff = w_down.shape[0]
    seq_tiles = seq // tm
    halo_per_tile = tm // SUBLANES
    n_halo = tokens // SUBLANES
    full = lambda shape: pl.BlockSpec(shape, lambda b, j: (0,) * len(shape), pipeline_mode=pl.Buffered(1))
    return pl.pallas_call(
        functools.partial(_ffn_kernel, d_ff=d_ff),
        grid=(tokens // seq, seq_tiles),
        in_specs=[
            pl.BlockSpec((tm, d_model), lambda b, j: (b * seq_tiles + j, 0)),
            pl.BlockSpec((SUBLANES, d_model),
                         lambda b, j: (jnp.maximum((b * seq_tiles + j) * halo_per_tile - 1, 0), 0)),
            pl.BlockSpec((SUBLANES, d_model),
                         lambda b, j: (jnp.minimum((b * seq_tiles + j + 1) * halo_per_tile, n_halo - 1), 0)),
            full(g.shape), full(w_up.shape), full(conv_w.shape), full(conv_b.shape), full(w_down.shape),
        ],
        out_specs=pl.BlockSpec((tm, d_model), lambda b, j: (b * seq_tiles + j, 0)),
        out_shape=jax.ShapeDtypeStruct((tokens, d_model), F32),
        scratch_shapes=[pltpu.VMEM((FFN_SLOTS, 2 * FFN_CHUNK // LANES, tm + 2 * SUBLANES, LANES), F32),
                        pltpu.VMEM((tm, d_ff), BF16)],
        compiler_params=_cparams("parallel", "parallel"),
        name="conv_ffn",
    )(x2, x2, x2, g, w_up, conv_w, conv_b, w_down)


def _rope_lane_tables(seq):
    half = ROPE_DIMS // 2
    pos = jnp.arange(seq, dtype=F32)
    inv = ROPE_THETA ** (-jnp.arange(0, ROPE_DIMS, 2, dtype=F32) / ROPE_DIMS)
    ang = pos[:, None] * inv[None, :]
    cos, sin = jnp.cos(ang), jnp.sin(ang)
    d = np.arange(LANES) % HEAD_DIM
    first, second = d < half, (d >= half) & (d < ROPE_DIMS)
    idx = d % half
    c_tab = jnp.where((first | second)[None, :], cos[:, idx], 1.0)
    a_tab = jnp.where(first[None, :], -sin[:, idx], 0.0)
    b_tab = jnp.where(second[None, :], sin[:, idx], 0.0)
    return jnp.stack([c_tab, a_tab, b_tab]).astype(F32)


def kernel(x, g_mix, w_in, gate_b, a_conv_w, a_conv_b, a_ln_g, a_ln_b, na_qn, na_kn, na_rpb, dil_qn, dil_kn, swa_qn, swa_kn, swa_sink, w_branch, w_out, g_ffn, w_up, ffn_conv_w, ffn_conv_b, w_down):
    bsz, seq, d_model = x.shape
    depth = w_in.shape[0]
    mix_w = d_model // N_BRANCH
    heads = mix_w // HEAD_DIM
    kv_heads = swa_sink.shape[1] // 2
    grp = heads // kv_heads
    n_gate = N_BRANCH * d_model
    assert mix_w % LANES == 0 and seq % (GRID_W * NA_ROWS) == 0
    assert all(w // (2 * d) == DIL_PAIRS[0][0] // 2 for w, d in DIL_PAIRS)

    rope_tab = _rope_lane_tables(seq)
    scale = HEAD_DIM ** -0.5 * LOG2_E
    tile_h = lambda v: jnp.tile(v.astype(F32), heads)
    x2 = x.reshape(bsz * seq, d_model)
    na_bias = _na_bias_table(na_rpb.astype(F32) * LOG2_E, seq // GRID_W)

    for l in range(depth):
        w_l = w_in[l]
        o_swk = w_l.shape[1] - 2 * kv_heads * HEAD_DIM
        rep_heads = lambda w: jnp.repeat(w.reshape(d_model, kv_heads, HEAD_DIM), grp, axis=1).reshape(d_model, mix_w)
        w_br = jnp.concatenate([
            w_l[:, n_gate:o_swk],
            rep_heads(w_l[:, o_swk:o_swk + kv_heads * HEAD_DIM]),
            rep_heads(w_l[:, o_swk + kv_heads * HEAD_DIM:]),
        ], axis=1).astype(BF16)
        gains = jnp.stack([tile_h(na_qn[l]) * scale, tile_h(na_kn[l]),
                           tile_h(dil_qn[l]) * scale, tile_h(dil_kn[l]),
                           tile_h(swa_qn[l]) * scale, tile_h(swa_kn[l]),
                           jnp.zeros((mix_w,), F32), jnp.zeros((mix_w,), F32)])

        a_u, na_q, na_k, na_v, dil_qkv, sw_q, sw_k, sw_v = _inproj(
            x2, g_mix[l][None, :], w_br, gains, rope_tab, seq=seq, tm=512)
        y_b = _na_attn(na_q, na_k, na_v, na_bias[l], seq=seq)
        y_c = _dilated_attn(dil_qkv, seq=seq, mix_w=mix_w)
        y_d = _swa_attn(swa_sink[l].astype(F32) * LOG2_E, sw_q, sw_k, sw_v, seq=seq)
        x2 = _merge(x2, g_mix[l][None, :], w_l[:, :n_gate].astype(BF16), gate_b[l].reshape(1, n_gate),
                    a_u, a_conv_w[l], a_conv_b[l][None, :], a_ln_g[l][None, :], a_ln_b[l][None, :],
                    (y_b, y_c, y_d), w_branch[l].astype(BF16), w_out[l].astype(BF16), seq=seq, tm=512)
        x2 = _ffn(x2, g_ffn[l][None, :], w_up[l].astype(BF16), ffn_conv_w[l], ffn_conv_b[l][None, :],
                  w_down[l].astype(BF16), seq=seq, tm=FFN_TM)
    return x2.reshape(bsz, seq, d_model)
```
